```python
import math
import jax, jax.numpy as jnp
from jax import lax
import numpy as np

D_MODEL = 1024
BATCH = 8
SEQ = 4096
DEPTH = 2

CHUNK = 64
N_MIXERS = 2
N_SSD_LAYERS = (DEPTH + 1) // 2
N_ATT_LAYERS = DEPTH // 2

SSD_EXPAND = 2
D_INNER = SSD_EXPAND * D_MODEL
SSD_HEAD_DIM = 64
SSD_HEADS = D_INNER // SSD_HEAD_DIM
SSD_GROUPS = 4
SSD_HEADS_PER_GROUP = SSD_HEADS // SSD_GROUPS
SSD_STATE = 128
SSD_CONV = 4
SSD_CONV_DIM = D_INNER + 2 * SSD_GROUPS * SSD_STATE
SSD_IN_DIM = 2 * D_INNER + 2 * SSD_GROUPS * SSD_STATE + SSD_HEADS

ATT_HEADS = 16
ATT_HEAD_DIM = D_MODEL // ATT_HEADS
LEFT_CHUNKS = 8
BAND = (LEFT_CHUNKS + 1) * CHUNK
MAX_REL = 128

N_EXPERTS = 16
N_EXPERT_GROUPS = 4
EXPERTS_PER_GROUP = N_EXPERTS // N_EXPERT_GROUPS
TOP_K = 2
D_FF_EXPERT = 512

DEEPNORM_ALPHA = (2.0 * DEPTH) ** 0.25
DEEPNORM_BETA = (8.0 * DEPTH) ** -0.25
LN_EPS = 1e-5
RMS_EPS = 1e-5

kernel_name = "hybrid_ssd_chunkattn_groupmoe_deepnorm"


def layer_norm(x, g, b):
    xf = x.astype(jnp.float32)
    mu = jnp.mean(xf, axis=-1, keepdims=True)
    var = jnp.mean(jnp.square(xf - mu), axis=-1, keepdims=True)
    out = (xf - mu) * lax.rsqrt(var + LN_EPS) * g.astype(jnp.float32) + b.astype(jnp.float32)
    return out.astype(x.dtype)


def causal_depthwise_conv(u, w, b):
    k_w = w.shape[0]
    s = u.shape[1]
    up = jnp.pad(u, ((0, 0), (k_w - 1, 0), (0, 0)))
    return sum(up[:, k:k + s] * w[k] for k in range(k_w)) + b


def ssd_mixer(x, w_in, conv_w, conv_b, dt_bias, a_log, d_skip, norm_w, w_out):
    bsz, s, _ = x.shape
    nc = s // CHUNK
    g, hg, p, n = SSD_GROUPS, SSD_HEADS_PER_GROUP, SSD_HEAD_DIM, SSD_STATE
    f32 = jnp.float32
    zxbcdt = x @ w_in
    z, xbc, dt = jnp.split(zxbcdt, [D_INNER, D_INNER + SSD_CONV_DIM], axis=-1)
    xbc = jax.nn.silu(causal_depthwise_conv(xbc, conv_w, conv_b))
    xs, bm, cm = jnp.split(xbc, [D_INNER, D_INNER + g * n], axis=-1)
    xs = xs.astype(f32).reshape(bsz, nc, CHUNK, g, hg, p)
    bm = bm.astype(f32).reshape(bsz, nc, CHUNK, g, n)
    cm = cm.astype(f32).reshape(bsz, nc, CHUNK, g, n)
    dt = jax.nn.softplus(dt.astype(f32) + dt_bias.astype(f32)).reshape(bsz, nc, CHUNK, g, hg)
    a = -jnp.exp(a_log.astype(f32)).reshape(g, hg)
    da = dt * a
    causal = jnp.tril(jnp.ones((CHUNK, CHUNK), dtype=bool))

    def step(h, inp):
        xc, bc, cc, dtc, dac = inp
        cum = jnp.cumsum(dac, axis=1)
        seg = cum[:, :, None] - cum[:, None, :]
        decay = jnp.exp(jnp.where(causal[None, :, :, None, None], seg, -jnp.inf))
        cb = jnp.einsum('bign,bjgn->bijg', cc, bc)
        w_ij = cb[..., None] * decay * dtc[:, None]
        y = jnp.einsum('bijgh,bjghp->bighp', w_ij, xc)
        y = y + jnp.einsum('bign,bghpn->bighp', cc, h) * jnp.exp(cum)[..., None]
        last = cum[:, -1]
        wj = jnp.exp(last[:, None] - cum) * dtc
        h = h * jnp.exp(last)[..., None, None] + jnp.einsum('bjgn,bjgh,bjghp->bghpn', bc, wj, xc)
        return h, y

    h0 = jnp.zeros((bsz, g, hg, p, n), f32)
    inputs = tuple(jnp.moveaxis(t, 1, 0) for t in (xs, bm, cm, dt, da))
    _, ys = lax.scan(step, h0, inputs)
    ys = jnp.moveaxis(ys, 0, 1)
    y = ys + d_skip.astype(f32).reshape(g, hg)[..., None] * xs
    y = y.reshape(bsz, s, D_INNER) * jax.nn.silu(z.astype(f32))
    yg = y.reshape(bsz, s, g, D_INNER // g)
    yg = yg * lax.rsqrt(jnp.mean(jnp.square(yg), axis=-1, keepdims=True) + RMS_EPS)
    y = yg.reshape(bsz, s, D_INNER) * norm_w.astype(f32)
    return y.astype(x.dtype) @ w_out


def chunked_attention(x, w_qkv, b_qkv, rel_table, w_o, b_o):
    bsz, s, _ = x.shape
    nc = s // CHUNK
    pad = LEFT_CHUNKS * CHUNK
    qkv = x @ w_qkv + b_qkv
    q, k, v = jnp.split(qkv, 3, axis=-1)
    q = q.reshape(bsz, s, ATT_HEADS, ATT_HEAD_DIM) * (ATT_HEAD_DIM ** -0.5)
    k = k.reshape(bsz, s, ATT_HEADS, ATT_HEAD_DIM)
    v = v.reshape(bsz, s, ATT_HEADS, ATT_HEAD_DIM)
    k_pad = jnp.pad(k, ((0, 0), (pad, 0), (0, 0), (0, 0)))
    v_pad = jnp.pad(v, ((0, 0), (pad, 0), (0, 0), (0, 0)))
    q_chunks = jnp.moveaxis(q.reshape(bsz, nc, CHUNK, ATT_HEADS, ATT_HEAD_DIM), 1, 0)
    qi = jnp.arange(CHUNK)
    kj = jnp.arange(BAND)
    dist = qi[:, None] + pad - kj[None, :]
    rel_idx = jnp.clip(dist, -MAX_REL, MAX_REL) + MAX_REL
    bias = jnp.transpose(rel_table[rel_idx], (2, 0, 1)).astype(jnp.float32)

    def one_chunk(args):
        qc, c = args
        start = c * CHUNK
        kb = lax.dynamic_slice_in_dim(k_pad, start, BAND, axis=1)
        vb = lax.dynamic_slice_in_dim(v_pad, start, BAND, axis=1)
        key_pos = start - pad + kj
        sc = jnp.einsum('bqhd,bkhd->bhqk', qc, kb).astype(jnp.float32) + bias
        sc = jnp.where((key_pos >= 0)[None, None, None, :], sc, -jnp.inf)
        pr = jax.nn.softmax(sc, axis=-1).astype(vb.dtype)
        return jnp.einsum('bhqk,bkhd->bqhd', pr, vb)

    out = lax.map(one_chunk, (q_chunks, jnp.arange(nc)))
    out = jnp.moveaxis(out, 0, 1).reshape(bsz, s, D_MODEL)
    return out @ w_o + b_o


def group_limited_moe(x, router_w, router_bias, w_gate, w_up, w_down):
    bsz, s, dm = x.shape
    t = x.reshape(-1, dm)
    scores = jax.nn.softmax((t @ router_w).astype(jnp.float32), axis=-1)
    sel = scores + router_bias.astype(jnp.float32)
    grp_top = lax.top_k(sel.reshape(-1, N_EXPERT_GROUPS, EXPERTS_PER_GROUP), TOP_K)[0]
    grp = jnp.argmax(jnp.sum(grp_top, axis=-1), axis=-1)
    expert_group = jnp.arange(N_EXPERTS) // EXPERTS_PER_GROUP
    masked = jnp.where(expert_group[None, :] == grp[:, None], sel, -jnp.inf)
    _, idx = lax.top_k(masked, TOP_K)
    gate = jnp.take_along_axis(scores, idx, axis=-1)
    gate = gate / jnp.sum(gate, axis=-1, keepdims=True)
    combine = jnp.sum(jax.nn.one_hot(idx, N_EXPERTS, dtype=jnp.float32) * gate[..., None], axis=1)
    out = jnp.zeros_like(t)
    for e in range(N_EXPERTS):
        h = jax.nn.silu(t @ w_gate[e]) * (t @ w_up[e])
        out = out + combine[:, e:e + 1].astype(t.dtype) * (h @ w_down[e])
    return out.reshape(bsz, s, dm)


def setup_inputs(seed: int = 0) -> dict:
    key = jax.random.key(seed)
    keys = jax.random.split(key, 24)
    f32 = jnp.float32

    def nrm(k, shape, scale):
        return jax.random.normal(k, shape, f32) * scale

    dt0 = jnp.exp(jax.random.uniform(keys[4], (N_SSD_LAYERS, SSD_HEADS), f32)
                  * (math.log(0.1) - math.log(1e-3)) + math.log(1e-3))
    ssm_dt_bias = dt0 + jnp.log(-jnp.expm1(-dt0))
    att_w_qkv = nrm(keys[9], (N_ATT_LAYERS, D_MODEL, 3 * D_MODEL), D_MODEL ** -0.5)
    att_w_qkv = att_w_qkv.at[..., 2 * D_MODEL:].multiply(DEEPNORM_BETA)
    return {
        "x": nrm(keys[0], (BATCH, SEQ, D_MODEL), 1.0),
        "ssm_w_in": nrm(keys[1], (N_SSD_LAYERS, D_MODEL, SSD_IN_DIM), D_MODEL ** -0.5),
        "ssm_conv_w": nrm(keys[2], (N_SSD_LAYERS, SSD_CONV, SSD_CONV_DIM), SSD_CONV ** -0.5),
        "ssm_conv_b": nrm(keys[3], (N_SSD_LAYERS, SSD_CONV_DIM), 0.02),
        "ssm_dt_bias": ssm_dt_bias,
        "ssm_a_log": jnp.log(jax.random.uniform(keys[5], (N_SSD_LAYERS, SSD_HEADS), f32, 1.0, 16.0)),
        "ssm_d": 1.0 + nrm(keys[6], (N_SSD_LAYERS, SSD_HEADS), 0.02),
        "ssm_norm_w": 1.0 + nrm(keys[7], (N_SSD_LAYERS, D_INNER), 0.02),
        "ssm_w_out": nrm(keys[8], (N_SSD_LAYERS, D_INNER, D_MODEL), D_INNER ** -0.5 * DEEPNORM_BETA),
        "att_w_qkv": att_w_qkv,
        "att_b_qkv": nrm(keys[10], (N_ATT_LAYERS, 3 * D_MODEL), 0.02),
        "att_rel_bias": nrm(keys[11], (N_ATT_LAYERS, 2 * MAX_REL + 1, ATT_HEADS), 0.5),
        "att_w_o": nrm(keys[12], (N_ATT_LAYERS, D_MODEL, D_MODEL), D_MODEL ** -0.5 * DEEPNORM_BETA),
        "att_b_o": nrm(keys[13], (N_ATT_LAYERS, D_MODEL), 0.02),
        "router_w": nrm(keys[14], (D_MODEL, N_EXPERTS), D_MODEL ** -0.5),
        "router_bias": nrm(keys[15], (N_EXPERTS,), 0.01),
        "moe_w_gate": nrm(keys[16], (DEPTH, N_EXPERTS, D_MODEL, D_FF_EXPERT), D_MODEL ** -0.5 * DEEPNORM_BETA),
        "moe_w_up": nrm(keys[17], (DEPTH, N_EXPERTS, D_MODEL, D_FF_EXPERT), D_MODEL ** -0.5 * DEEPNORM_BETA),
        "moe_w_down": nrm(keys[18], (DEPTH, N_EXPERTS, D_FF_EXPERT, D_MODEL), D_FF_EXPERT ** -0.5 * DEEPNORM_BETA),
        "ln_mix_g": 1.0 + nrm(keys[19], (DEPTH, D_MODEL), 0.02),
        "ln_mix_b": nrm(keys[20], (DEPTH, D_MODEL), 0.02),
        "ln_ffn_g": 1.0 + nrm(keys[21], (DEPTH, D_MODEL), 0.02),
        "ln_ffn_b": nrm(keys[22], (DEPTH, D_MODEL), 0.02),
    }


def reference(x, ssm_w_in, ssm_conv_w, ssm_conv_b, ssm_dt_bias, ssm_a_log, ssm_d,
              ssm_norm_w, ssm_w_out, att_w_qkv, att_b_qkv, att_rel_bias, att_w_o, att_b_o,
              router_w, router_bias, moe_w_gate, moe_w_up, moe_w_down,
              ln_mix_g, ln_mix_b, ln_ffn_g, ln_ffn_b):
    for i in range(DEPTH):
        j = i // N_MIXERS
        if i % N_MIXERS == 0:
            mix = ssd_mixer(x, ssm_w_in[j], ssm_conv_w[j], ssm_conv_b[j], ssm_dt_bias[j],
                            ssm_a_log[j], ssm_d[j], ssm_norm_w[j], ssm_w_out[j])
        else:
            mix = chunked_attention(x, att_w_qkv[j], att_b_qkv[j], att_rel_bias[j],
                                    att_w_o[j], att_b_o[j])
        x = layer_norm(DEEPNORM_ALPHA * x + mix, ln_mix_g[i], ln_mix_b[i])
        ffn = group_limited_moe(x, router_w, router_bias, moe_w_gate[i], moe_w_up[i], moe_w_down[i])
        x = layer_norm(DEEPNORM_ALPHA * x + ffn, ln_ffn_g[i], ln_ffn_b[i])
    return x
```

```python
import functools
import math

import jax
import jax.numpy as jnp
from jax import lax
from jax.experimental import pallas as pl
from jax.experimental.pallas import tpu as pltpu

F32 = jnp.float32
BF16 = jnp.bfloat16
I32 = jnp.int32

D_MODEL = 1024
DEPTH = 2
CHUNK = 64
D_INNER = 2048
SSD_HEAD_DIM = 64
SSD_HEADS = 32
SSD_GROUPS = 4
SSD_HEADS_PER_GROUP = 8
SSD_STATE = 128
SSD_CONV = 4
SSD_BC = SSD_GROUPS * SSD_STATE
SSD_CONV_DIM = D_INNER + 2 * SSD_BC
ATT_HEADS = 16
ATT_HEAD_DIM = 64
LEFT_CHUNKS = 8
MAX_REL = 128
N_EXPERTS = 16
N_EXPERT_GROUPS = 4
EXPERTS_PER_GROUP = 4
D_FF = 512
ALPHA = (2.0 * DEPTH) ** 0.25
LN_EPS = 1e-5
RMS_EPS = 1e-5

PAIRS = ((0, 1), (0, 2), (0, 3), (1, 2), (1, 3), (2, 3))
N_CLASSES = N_EXPERT_GROUPS * len(PAIRS)
CLASS_ROWS = 32

LANES = 128
VMEM_LIMIT = 56 * 1024 * 1024

TM_PROJ = 512
SSD_BLOCK = 256
ATT_BLOCK = 128
ATT_KBLOCKS = 5
RANK_BLOCK = 1024
ROW_TILE = 256
DMA_CHUNK = 1024
NEG_BIG = -1e30


def _cparams(sem):
    return pltpu.CompilerParams(dimension_semantics=sem, vmem_limit_bytes=VMEM_LIMIT)


def _split3(a):
    hi = a.astype(BF16)
    r1 = a - hi.astype(F32)
    mid = r1.astype(BF16)
    lo = (r1 - mid.astype(F32)).astype(BF16)
    return hi, mid, lo


def _dot(a, b):
    return jnp.dot(a, b, preferred_element_type=F32)


def _dot_nt(a, b):
    return lax.dot_general(a, b, (((1,), (1,)), ((), ())), preferred_element_type=F32)


def _dot_tn(a, b):
    return lax.dot_general(a, b, (((0,), (0,)), ((), ())), preferred_element_type=F32)


def _sel_right(a, m01):
    hi, mid, lo = _split3(a)
    return _dot(hi, m01) + (_dot(mid, m01) + _dot(lo, m01))


def _sel_left(m01, a):
    hi, mid, lo = _split3(a)
    return _dot(m01, hi) + (_dot(m01, mid) + _dot(m01, lo))


def _sel_nt(m01, a):
    hi, mid, lo = _split3(a)
    return _dot_nt(m01, hi) + (_dot_nt(m01, mid) + _dot_nt(m01, lo))


def _dot_f32(a, b):
    ah, am, al = _split3(a)
    bh, bm, bl = _split3(b)
    return (_dot(ah, bh) + (_dot(ah, bm) + _dot(am, bh))
            + (_dot(am, bm) + _dot(ah, bl) + _dot(al, bh)))


def _dot_f32_nt(a, b):
    ah, am, al = _split3(a)
    bh, bm, bl = _split3(b)
    return (_dot_nt(ah, bh) + (_dot_nt(ah, bm) + _dot_nt(am, bh))
            + (_dot_nt(am, bm) + _dot_nt(ah, bl) + _dot_nt(al, bh)))


def _sigmoid(x):
    return 1.0 / (1.0 + jnp.exp(-x))


def _silu(x):
    return x * _sigmoid(x)


def _layer_norm(v, g, b):
    mu = jnp.mean(v, axis=-1, keepdims=True)
    c = v - mu
    var = jnp.mean(c * c, axis=-1, keepdims=True)
    return c * lax.rsqrt(var + LN_EPS) * g + b


def _in_proj_kernel(x_ref, wz_ref, wxbc_ref, wdt_ref, z_ref, xbc_ref, dt_ref):
    x = x_ref[...]
    xb = x.astype(BF16)
    nchunk = 512
    for j in range(D_INNER // nchunk):
        sl = slice(j * nchunk, (j + 1) * nchunk)
        z_ref[:, sl] = _dot(xb, wz_ref[:, sl]).astype(BF16)
    for j in range(SSD_CONV_DIM // nchunk):
        sl = slice(j * nchunk, (j + 1) * nchunk)
        xbc_ref[:, sl] = _dot(xb, wxbc_ref[:, sl]).astype(BF16)
    dt_ref[...] = _dot_f32(x, wdt_ref[...])


def _in_proj(x2d, wz, wxbc, wdt):
    t = x2d.shape[0]
    tm = min(TM_PROJ, t)
    const = lambda i: (0, 0)
    return pl.pallas_call(
        _in_proj_kernel,
        grid=(t // tm,),
        in_specs=[
            pl.BlockSpec((tm, D_MODEL), lambda i: (i, 0)),
            pl.BlockSpec((D_MODEL, D_INNER), const),
            pl.BlockSpec((D_MODEL, SSD_CONV_DIM), const),
            pl.BlockSpec((D_MODEL, SSD_HEADS), const),
        ],
        out_specs=[
            pl.BlockSpec((tm, D_INNER), lambda i: (i, 0)),
            pl.BlockSpec((tm, SSD_CONV_DIM), lambda i: (i, 0)),
            pl.BlockSpec((tm, SSD_HEADS), lambda i: (i, 0)),
        ],
        out_shape=[
            jax.ShapeDtypeStruct((t, D_INNER), BF16),
            jax.ShapeDtypeStruct((t, SSD_CONV_DIM), BF16),
            jax.ShapeDtypeStruct((t, SSD_HEADS), F32),
        ],
        compiler_params=_cparams(("parallel",)),
        name="ssd_in_proj",
    )(x2d, wz, wxbc, wdt)


def _ssd_kernel(z_ref, xbc_ref, dt_ref, cw_ref, cb_ref, dtb_ref, alog_ref, dexp_ref, nw_ref,
                y_ref, h_ref, carry_ref, ext_ref, xs_ref, b_ref, c_ref, yacc_ref):
    lb = z_ref.shape[0]
    nchunks = lb // CHUNK
    gp = SSD_HEADS_PER_GROUP * SSD_HEAD_DIM

    @pl.when(pl.program_id(1) == 0)
    def _():
        h_ref[...] = jnp.zeros_like(h_ref)
        carry_ref[...] = jnp.zeros_like(carry_ref)

    cchunk = 512
    for j in range(SSD_CONV_DIM // cchunk):
        sl = slice(j * cchunk, (j + 1) * cchunk)
        u = xbc_ref[:, sl].astype(F32)
        ext_ref[0:8, :] = carry_ref[:, sl]
        ext_ref[8:8 + lb, :] = u
        carry_ref[:, sl] = u[lb - 8:lb, :]
        acc = cb_ref[:, sl] + ext_ref[5:5 + lb, :] * cw_ref[0:1, sl]
        for k in range(1, SSD_CONV):
            acc = acc + ext_ref[5 + k:5 + k + lb, :] * cw_ref[k:k + 1, sl]
        act = _silu(acc)
        if j < D_INNER // cchunk:
            xs_ref[:, sl] = act
        elif j < (D_INNER + SSD_BC) // cchunk:
            b_ref[...] = act.astype(BF16)
        else:
            c_ref[...] = act.astype(BF16)

    dt_all = dt_ref[...] + dtb_ref[...]
    dt_all = jnp.maximum(dt_all, 0.0) + jnp.log1p(jnp.exp(-jnp.abs(dt_all)))
    a_neg = -jnp.exp(alog_ref[...])
    da_all = dt_all * a_neg

    r = lax.broadcasted_iota(I32, (CHUNK, CHUNK), 0)
    c = lax.broadcasted_iota(I32, (CHUNK, CHUNK), 1)
    causal = r >= c
    tril = causal.astype(BF16)
    eye_h = (lax.broadcasted_iota(I32, (SSD_HEADS, SSD_HEADS), 0)
             == lax.broadcasted_iota(I32, (SSD_HEADS, SSD_HEADS), 1)).astype(BF16)
    eh = lax.broadcasted_iota(I32, (SSD_HEADS, D_INNER), 0)
    ec = lax.broadcasted_iota(I32, (SSD_HEADS, D_INNER), 1)
    expand = ((ec // SSD_HEAD_DIM) == eh).astype(BF16)
    lane = lax.broadcasted_iota(I32, (CHUNK, LANES), 1)
    lo_half = lane < SSD_HEAD_DIM

    for ci in range(nchunks):
        rows = slice(ci * CHUNK, (ci + 1) * CHUNK)
        dt_c = dt_all[rows, :]
        da_c = da_all[rows, :]
        cum = _sel_left(tril, da_c)
        cum_t = _sel_nt(eye_h, cum)
        dt_t = _sel_nt(eye_h, dt_c)
        last = cum[CHUNK - 1:CHUNK, :]
        ecum_x = _sel_right(jnp.exp(cum), expand)
        wj_x = _sel_right(jnp.exp(last - cum) * dt_c, expand)
        elast_x = _sel_right(jnp.exp(last), expand)

        xs_c = xs_ref[rows, :]
        xs_b = xs_c.astype(BF16)
        xw_b = (xs_c * wj_x).astype(BF16)
        b_c = b_ref[rows, :]
        c_c = c_ref[rows, :]

        for g in range(SSD_GROUPS):
            ns = slice(g * SSD_STATE, (g + 1) * SSD_STATE)
            gs = slice(g * gp, (g + 1) * gp)
            cb = _dot_nt(c_c[:, ns], b_c[:, ns])
            h_g = h_ref[g]
            y_inter = _dot(c_c[:, ns], h_g.astype(BF16)) * ecum_x[:, gs]
            for hp in range(SSD_HEADS_PER_GROUP // 2):
                col = g * gp + hp * LANES
                xpair = xs_b[:, col:col + LANES]
                y_pair = jnp.zeros((CHUNK, LANES), F32)
                for sub in range(2):
                    h = g * SSD_HEADS_PER_GROUP + hp * 2 + sub
                    seg = cum[:, h:h + 1] - cum_t[h:h + 1, :]
                    decay = jnp.exp(jnp.where(causal, seg, -jnp.inf))
                    w = (cb * decay * dt_t[h:h + 1, :]).astype(BF16)
                    keep = lo_half if sub == 0 else jnp.logical_not(lo_half)
                    xh = jnp.where(keep, xpair, jnp.zeros_like(xpair))
                    y_pair = y_pair + _dot(w, xh)
                yacc_ref[rows, col:col + LANES] = (
                    y_pair + y_inter[:, hp * LANES:(hp + 1) * LANES])
            h_ref[g] = h_g * elast_x[:, gs] + _dot_tn(b_c[:, ns], xw_b[:, gs])

    for g in range(SSD_GROUPS):
        gs = slice(g * gp, (g + 1) * gp)
        y = yacc_ref[:, gs] + dexp_ref[:, gs] * xs_ref[:, gs]
        y = y * _silu(z_ref[:, gs].astype(F32))
        ms = jnp.mean(y * y, axis=-1, keepdims=True)
        y_ref[:, gs] = (y * lax.rsqrt(ms + RMS_EPS) * nw_ref[:, gs]).astype(BF16)


def _ssd_core(z, xbc, dt, conv_w, conv_b, dt_bias, a_log, d_exp, norm_w, bsz, seq):
    lb = min(SSD_BLOCK, seq)
    nblk = seq // lb
    row = lambda b, c: (b * nblk + c, 0)
    const = lambda b, c: (0, 0)
    return pl.pallas_call(
        _ssd_kernel,
        grid=(bsz, nblk),
        in_specs=[
            pl.BlockSpec((lb, D_INNER), row),
            pl.BlockSpec((lb, SSD_CONV_DIM), row),
            pl.BlockSpec((lb, SSD_HEADS), row),
            pl.BlockSpec((SSD_CONV, SSD_CONV_DIM), const),
            pl.BlockSpec((1, SSD_CONV_DIM), const),
            pl.BlockSpec((1, SSD_HEADS), const),
            pl.BlockSpec((1, SSD_HEADS), const),
            pl.BlockSpec((1, D_INNER), const),
            pl.BlockSpec((1, D_INNER), const),
        ],
        out_specs=pl.BlockSpec((lb, D_INNER), row),
        out_shape=jax.ShapeDtypeStruct((bsz * seq, D_INNER), BF16),
        scratch_shapes=[
            pltpu.VMEM((SSD_GROUPS, SSD_STATE, SSD_HEADS_PER_GROUP * SSD_HEAD_DIM), F32),
            pltpu.VMEM((8, SSD_CONV_DIM), F32),
            pltpu.VMEM((lb + 8, 512), F32),
            pltpu.VMEM((lb, D_INNER), F32),
            pltpu.VMEM((lb, SSD_BC), BF16),
            pltpu.VMEM((lb, SSD_BC), BF16),
            pltpu.VMEM((lb, D_INNER), F32),
        ],
        compiler_params=_cparams(("parallel", "arbitrary")),
        name="ssd_core",
    )(z, xbc, dt, conv_w, conv_b, dt_bias, a_log, d_exp, norm_w)


def _route(x, rwt_ref, rb_ref):
    logits = _dot_f32_nt(rwt_ref[...], x)
    m = jnp.max(logits, axis=0, keepdims=True)
    e = jnp.exp(logits - m)
    scores = e / jnp.sum(e, axis=0, keepdims=True)
    sel = scores + rb_ref[...]
    s = [sel[i:i + 1, :] for i in range(N_EXPERTS)]
    gsum = []
    for g in range(N_EXPERT_GROUPS):
        v = s[g * EXPERTS_PER_GROUP:(g + 1) * EXPERTS_PER_GROUP]
        best = v[0] + v[1]
        for (i, j) in PAIRS[1:]:
            best = jnp.maximum(best, v[i] + v[j])
        gsum.append(best)
    grp = jnp.zeros_like(gsum[0], dtype=I32)
    best = gsum[0]
    for g in range(1, N_EXPERT_GROUPS):
        better = gsum[g] > best
        grp = jnp.where(better, g, grp)
        best = jnp.where(better, gsum[g], best)
    v = []
    for k in range(EXPERTS_PER_GROUP):
        vk = s[k]
        for g in range(1, N_EXPERT_GROUPS):
            vk = jnp.where(grp == g, s[g * EXPERTS_PER_GROUP + k], vk)
        v.append(vk)
    i1 = jnp.zeros_like(grp)
    best = v[0]
    for k in range(1, EXPERTS_PER_GROUP):
        better = v[k] > best
        i1 = jnp.where(better, k, i1)
        best = jnp.where(better, v[k], best)
    i2 = jnp.full_like(grp, -1)
    best = jnp.full_like(v[0], -jnp.inf)
    for k in range(EXPERTS_PER_GROUP):
        better = jnp.logical_and(i1 != k, jnp.logical_or(v[k] > best, i2 < 0))
        i2 = jnp.where(better, k, i2)
        best = jnp.where(better, v[k], best)
    lo = jnp.minimum(i1, i2)
    hi = jnp.maximum(i1, i2)
    pair = jnp.where(lo == 0, hi - 1, jnp.where(lo == 1, hi + 1, 5))
    return grp * len(PAIRS) + pair


def _proj_ln_route_kernel(a_ref, w_ref, bias_ref, xres_ref, g_ref, b_ref, rwt_ref, rb_ref,
                          xo_ref, cls_ref):
    acc = _dot(a_ref[...], w_ref[...]) + bias_ref[...]
    v = ALPHA * xres_ref[...] + acc
    out = _layer_norm(v, g_ref[...], b_ref[...])
    xo_ref[...] = out
    cls_ref[...] = _route(out, rwt_ref, rb_ref)


def _proj_ln_route(a, w, bias, xres, g, b, rwt, rb):
    t, k = a.shape
    tm = min(TM_PROJ, t)
    const = lambda i: (0, 0)
    return pl.pallas_call(
        _proj_ln_route_kernel,
        grid=(t // tm,),
        in_specs=[
            pl.BlockSpec((tm, k), lambda i: (i, 0)),
            pl.BlockSpec((k, D_MODEL), const),
            pl.BlockSpec((1, D_MODEL), const),
            pl.BlockSpec((tm, D_MODEL), lambda i: (i, 0)),
            pl.BlockSpec((1, D_MODEL), const),
            pl.BlockSpec((1, D_MODEL), const),
            pl.BlockSpec((N_EXPERTS, D_MODEL), const),
            pl.BlockSpec((N_EXPERTS, 1), const),
        ],
        out_specs=[
            pl.BlockSpec((tm, D_MODEL), lambda i: (i, 0)),
            pl.BlockSpec((1, tm), lambda i: (0, i)),
        ],
        out_shape=[
            jax.ShapeDtypeStruct((t, D_MODEL), F32),
            jax.ShapeDtypeStruct((1, t), I32),
        ],
        compiler_params=_cparams(("parallel",)),
        name="proj_ln_route",
    )(a, w, bias, xres, g, b, rwt, rb)


def _rank_kernel(cls_ref, rank_ref, cnt_ref, run_ref, upper_ref):
    tb = cls_ref.shape[1]

    @pl.when(pl.program_id(0) == 0)
    def _():
        run_ref[...] = jnp.zeros_like(run_ref)
        r = lax.broadcasted_iota(I32, (tb, tb), 0)
        c = lax.broadcasted_iota(I32, (tb, tb), 1)
        upper_ref[...] = (r < c).astype(BF16)

    cls = cls_ref[...]
    cid = lax.broadcasted_iota(I32, (CLASS_ROWS, tb), 0)
    onehot = cid == cls
    oh_b = onehot.astype(BF16)
    before = _dot(oh_b, upper_ref[...])
    run = run_ref[...]
    rank = jnp.sum(jnp.where(onehot, before + run, 0.0), axis=0, keepdims=True)
    rank_ref[...] = rank.astype(I32)
    run = run + jnp.sum(onehot.astype(F32), axis=1, keepdims=True)
    run_ref[...] = run
    cnt_ref[...] = run.astype(I32)


def _rank_in_class(cls):
    t = cls.shape[1]
    tb = min(RANK_BLOCK, t)
    return pl.pallas_call(
        _rank_kernel,
        grid=(t // tb,),
        in_specs=[pl.BlockSpec((1, tb), lambda i: (0, i))],
        out_specs=[
            pl.BlockSpec((1, tb), lambda i: (0, i)),
            pl.BlockSpec((CLASS_ROWS, 1), lambda i: (0, 0)),
        ],
        out_shape=[
            jax.ShapeDtypeStruct((1, t), I32),
            jax.ShapeDtypeStruct((CLASS_ROWS, 1), I32),
        ],
        scratch_shapes=[
            pltpu.VMEM((CLASS_ROWS, 1), F32),
            pltpu.VMEM((tb, tb), BF16),
        ],
        compiler_params=_cparams(("arbitrary",)),
        name="rank_in_class",
    )(cls)


def _row_copy(src_hbm, dst_hbm, sem, s, d):
    return pltpu.make_async_copy(src_hbm.at[s], dst_hbm.at[d], sem)


def _move_rows_kernel(scatter, pos_ref, src_hbm, *rest):
    dst_hbm, sem = rest[-2], rest[-1]
    n = pos_ref.shape[2]
    base = pl.program_id(0) * n

    def issue(i, carry):
        p = pos_ref[0, 0, i]
        if scatter:
            _row_copy(src_hbm, dst_hbm, sem, base + i, p).start()
        else:
            _row_copy(src_hbm, dst_hbm, sem, p, base + i).start()
        return carry

    lax.fori_loop(0, n, issue, 0)

    def drain(i, carry):
        _row_copy(src_hbm, dst_hbm, sem, 0, 0).wait()
        return carry

    lax.fori_loop(0, n, drain, 0)


def _scatter_rows(src, pos, n_rows_out):
    t = src.shape[0]
    n = min(DMA_CHUNK, t)
    zeros = jnp.zeros((n_rows_out,) + src.shape[1:], src.dtype)
    return pl.pallas_call(
        functools.partial(_move_rows_kernel, True),
        grid=(t // n,),
        in_specs=[
            pl.BlockSpec((1, 1, n), lambda i: (i, 0, 0), memory_space=pltpu.SMEM),
            pl.BlockSpec(memory_space=pl.ANY),
            pl.BlockSpec(memory_space=pl.ANY),
        ],
        out_specs=pl.BlockSpec(memory_space=pl.ANY),
        out_shape=jax.ShapeDtypeStruct(zeros.shape, src.dtype),
        scratch_shapes=[pltpu.SemaphoreType.DMA(())],
        input_output_aliases={2: 0},
        compiler_params=pltpu.CompilerParams(dimension_semantics=("arbitrary",),
                                             has_side_effects=True),
        name="scatter_rows",
    )(pos.reshape(t // n, 1, n), src, zeros)


def _gather_rows(src, pos):
    t = pos.shape[0]
    n = min(DMA_CHUNK, t)
    return pl.pallas_call(
        functools.partial(_move_rows_kernel, False),
        grid=(t // n,),
        in_specs=[
            pl.BlockSpec((1, 1, n), lambda i: (i, 0, 0), memory_space=pltpu.SMEM),
            pl.BlockSpec(memory_space=pl.ANY),
        ],
        out_specs=pl.BlockSpec(memory_space=pl.ANY),
        out_shape=jax.ShapeDtypeStruct((t,) + src.shape[1:], src.dtype),
        scratch_shapes=[pltpu.SemaphoreType.DMA(())],
        compiler_params=pltpu.CompilerParams(dimension_semantics=("arbitrary",),
                                             has_side_effects=True),
        name="gather_rows",
    )(pos.reshape(t // n, 1, n), src)


def _ffn_kernel(e1_ref, e2_ref, valid_ref, x_ref, wg1_ref, wu1_ref, wd1_ref,
                wg2_ref, wu2_ref, wd2_ref, rw_ref, g_ref, b_ref, o_ref):
    i = pl.program_id(0)

    @pl.when(valid_ref[i] > 0)
    def _():
        x = x_ref[...]
        xb = x.astype(BF16)
        logits = _dot_f32(x, rw_ref[...])
        m = jnp.max(logits, axis=-1, keepdims=True)
        e = jnp.exp(logits - m)
        scores = e / jnp.sum(e, axis=-1, keepdims=True)
        lane = lax.broadcasted_iota(I32, scores.shape, 1)
        s1 = jnp.sum(jnp.where(lane == e1_ref[i], scores, 0.0), axis=-1, keepdims=True)
        s2 = jnp.sum(jnp.where(lane == e2_ref[i], scores, 0.0), axis=-1, keepdims=True)
        den = s1 + s2
        h1 = _silu(_dot(xb, wg1_ref[...])) * _dot(xb, wu1_ref[...])
        h2 = _silu(_dot(xb, wg2_ref[...])) * _dot(xb, wu2_ref[...])
        ffn = ((s1 / den) * _dot(h1.astype(BF16), wd1_ref[...])
               + (s2 / den) * _dot(h2.astype(BF16), wd2_ref[...]))
        o_ref[...] = _layer_norm(ALPHA * x + ffn, g_ref[...], b_ref[...])

    @pl.when(valid_ref[i] == 0)
    def _():
        o_ref[...] = jnp.zeros_like(o_ref)


def _grouped_ffn(xs, tile_e1, tile_e2, tile_valid, wg, wu, wd, rw, g, b):
    r = xs.shape[0]
    rt = ROW_TILE
    nt = r // rt
    const = lambda i, *_: (0, 0)
    w1 = lambda i, e1, e2, v: (e1[i], 0, 0)
    w2 = lambda i, e1, e2, v: (e2[i], 0, 0)
    grid_spec = pltpu.PrefetchScalarGridSpec(
        num_scalar_prefetch=3,
        grid=(nt,),
        in_specs=[
            pl.BlockSpec((rt, D_MODEL), lambda i, *_: (i, 0)),
            pl.BlockSpec((None, D_MODEL, D_FF), w1),
            pl.BlockSpec((None, D_MODEL, D_FF), w1),
            pl.BlockSpec((None, D_FF, D_MODEL), w1),
            pl.BlockSpec((None, D_MODEL, D_FF), w2),
            pl.BlockSpec((None, D_MODEL, D_FF), w2),
            pl.BlockSpec((None, D_FF, D_MODEL), w2),
            pl.BlockSpec((D_MODEL, N_EXPERTS), const),
            pl.BlockSpec((1, D_MODEL), const),
            pl.BlockSpec((1, D_MODEL), const),
        ],
        out_specs=pl.BlockSpec((rt, D_MODEL), lambda i, *_: (i, 0)),
    )
    return pl.pallas_call(
        _ffn_kernel,
        grid_spec=grid_spec,
        out_shape=jax.ShapeDtypeStruct((r, D_MODEL), F32),
        compiler_params=_cparams(("arbitrary",)),
        name="grouped_ffn",
    )(tile_e1, tile_e2, tile_valid, xs, wg, wu, wd, wg, wu, wd, rw, g, b)


def _moe_layer(x, cls, wg, wu, wd, rw, g, b):
    t = x.shape[0]
    rt = ROW_TILE
    nt = t // rt + N_CLASSES
    rank, counts = _rank_in_class(cls)
    counts = counts[:N_CLASSES, 0]
    padded = ((counts + rt - 1) // rt) * rt
    ends = jnp.cumsum(padded)
    offs = ends - padded
    pos = offs[cls[0]] + rank[0]
    tile_start = jnp.arange(nt, dtype=I32) * rt
    tile_cls = jnp.searchsorted(ends, tile_start, side="right").astype(I32)
    tile_valid = (tile_start < ends[-1]).astype(I32)
    last_cls = jnp.max(jnp.where(tile_valid > 0, tile_cls, 0))
    tile_cls = jnp.where(tile_valid > 0, tile_cls, last_cls)
    grp = tile_cls // len(PAIRS)
    pair = tile_cls % len(PAIRS)
    pair_lo = jnp.array([p[0] for p in PAIRS], I32)[pair]
    pair_hi = jnp.array([p[1] for p in PAIRS], I32)[pair]
    tile_e1 = grp * EXPERTS_PER_GROUP + pair_lo
    tile_e2 = grp * EXPERTS_PER_GROUP + pair_hi

    rows = x.reshape(t, D_MODEL // LANES, LANES)
    xs = _scatter_rows(rows, pos, nt * rt)
    ys = _grouped_ffn(xs.reshape(nt * rt, D_MODEL), tile_e1, tile_e2, tile_valid,
                      wg, wu, wd, rw, g, b)
    out = _gather_rows(ys.reshape(nt * rt, D_MODEL // LANES, LANES), pos)
    return out.reshape(t, D_MODEL)


def _qkv_kernel(x_ref, w_ref, b_ref, q_ref, k_ref, v_ref):
    xb = x_ref[...].astype(BF16)
    nchunk = 512
    for o, ref in enumerate((q_ref, k_ref, v_ref)):
        for j in range(D_MODEL // nchunk):
            sl = slice(j * nchunk, (j + 1) * nchunk)
            wsl = slice(o * D_MODEL + j * nchunk, o * D_MODEL + (j + 1) * nchunk)
            r = _dot(xb, w_ref[:, wsl]) + b_ref[:, wsl]
            if o == 0:
                r = r * (ATT_HEAD_DIM ** -0.5)
            ref[:, sl] = r.astype(BF16)


def _qkv_proj(x2d, w, b):
    t = x2d.shape[0]
    tm = min(TM_PROJ, t)
    const = lambda i: (0, 0)
    row = pl.BlockSpec((tm, D_MODEL), lambda i: (i, 0))
    return pl.pallas_call(
        _qkv_kernel,
        grid=(t // tm,),
        in_specs=[row, pl.BlockSpec((D_MODEL, 3 * D_MODEL), const),
                  pl.BlockSpec((1, 3 * D_MODEL), const)],
        out_specs=[row, row, row],
        out_shape=[jax.ShapeDtypeStruct((t, D_MODEL), BF16)] * 3,
        compiler_params=_cparams(("parallel",)),
        name="qkv_proj",
    )(x2d, w, b)


def _attn_kernel(q_ref, *rest):
    k_refs = rest[:ATT_KBLOCKS]
    v_refs = rest[ATT_KBLOCKS:2 * ATT_KBLOCKS]
    bias_ref = rest[2 * ATT_KBLOCKS]
    o_ref = rest[2 * ATT_KBLOCKS + 1]
    qi = pl.program_id(1)
    nkeys = ATT_KBLOCKS * ATT_BLOCK
    first_valid = jnp.maximum((ATT_KBLOCKS - 1 - qi) * ATT_BLOCK, 0)
    col = lax.broadcasted_iota(I32, (ATT_BLOCK, nkeys), 1)
    key_ok = col >= first_valid
    lane = lax.broadcasted_iota(I32, (ATT_BLOCK, LANES), 1)
    lo_half = lane < ATT_HEAD_DIM
    for hp in range(ATT_HEADS // 2):
        cs = slice(hp * LANES, (hp + 1) * LANES)
        qp = q_ref[:, cs]
        out_pair = jnp.zeros((ATT_BLOCK, LANES), F32)
        for sub in range(2):
            keep = lo_half if sub == 0 else jnp.logical_not(lo_half)
            qm = jnp.where(keep, qp, jnp.zeros_like(qp))
            s = jnp.concatenate([_dot_nt(qm, k_refs[d][:, cs]) for d in range(ATT_KBLOCKS)],
                                axis=1)
            s = s + bias_ref[hp * 2 + sub]
            s = jnp.where(key_ok, s, NEG_BIG)
            m = jnp.max(s, axis=-1, keepdims=True)
            p = jnp.exp(s - m)
            l = jnp.sum(p, axis=-1, keepdims=True)
            pb = p.astype(BF16)
            o = _dot(pb[:, 0:ATT_BLOCK], v_refs[0][:, cs])
            for d in range(1, ATT_KBLOCKS):
                o = o + _dot(pb[:, d * ATT_BLOCK:(d + 1) * ATT_BLOCK], v_refs[d][:, cs])
            out_pair = jnp.where(keep, o / l, out_pair)
        o_ref[:, cs] = out_pair.astype(BF16)


def _band_attention(q, k, v, bias, bsz, seq):
    nq = seq // ATT_BLOCK
    row = lambda b, i: (b * nq + i, 0)

    def kv_spec(d):
        shift = ATT_KBLOCKS - 1 - d
        return pl.BlockSpec((ATT_BLOCK, D_MODEL),
                            lambda b, i: (b * nq + jnp.maximum(i - shift, 0), 0))

    kspecs = [kv_spec(d) for d in range(ATT_KBLOCKS)]
    return pl.pallas_call(
        _attn_kernel,
        grid=(bsz, nq),
        in_specs=[pl.BlockSpec((ATT_BLOCK, D_MODEL), row)] + kspecs + kspecs + [
            pl.BlockSpec((ATT_HEADS, ATT_BLOCK, ATT_KBLOCKS * ATT_BLOCK), lambda b, i: (0, 0, 0))],
        out_specs=pl.BlockSpec((ATT_BLOCK, D_MODEL), row),
        out_shape=jax.ShapeDtypeStruct((bsz * seq, D_MODEL), BF16),
        compiler_params=_cparams(("parallel", "arbitrary")),
        name="band_attention",
    )(q, *([k] * ATT_KBLOCKS), *([v] * ATT_KBLOCKS), bias)


def _attention_bias(rel_table):
    nkeys = ATT_KBLOCKS * ATT_BLOCK
    r = jnp.arange(ATT_BLOCK)[:, None]
    j = jnp.arange(nkeys)[None, :]
    dist = r + LEFT_CHUNKS * CHUNK - j
    idx = jnp.clip(dist, -MAX_REL, MAX_REL) + MAX_REL
    bias = jnp.transpose(rel_table[idx], (2, 0, 1)).astype(F32)
    first = r // CHUNK
    in_band = jnp.logical_and(j >= first * CHUNK, j < first * CHUNK + (LEFT_CHUNKS + 1) * CHUNK)
    return jnp.where(in_band[None], bias, NEG_BIG)


def kernel(x, ssm_w_in, ssm_conv_w, ssm_conv_b, ssm_dt_bias, ssm_a_log, ssm_d, ssm_norm_w, ssm_w_out, att_w_qkv, att_b_qkv, att_rel_bias, att_w_o, att_b_o, router_w, router_bias, moe_w_gate, moe_w_up, moe_w_down, ln_mix_g, ln_mix_b, ln_ffn_g, ln_ffn_b):
    bsz, seq, _ = x.shape
    t = bsz * seq
    xt = x.reshape(t, D_MODEL)
    rwt = router_w.T
    rb = router_bias.reshape(N_EXPERTS, 1)
    row = lambda a: a.reshape(1, -1)
    wg = moe_w_gate.astype(BF16)
    wu = moe_w_up.astype(BF16)
    wd = moe_w_down.astype(BF16)

    w_in = ssm_w_in[0]
    wz = w_in[:, :D_INNER].astype(BF16)
    wxbc = w_in[:, D_INNER:D_INNER + SSD_CONV_DIM].astype(BF16)
    wdt = w_in[:, D_INNER + SSD_CONV_DIM:]
    z, xbc, dt = _in_proj(xt, wz, wxbc, wdt)
    d_exp = jnp.repeat(ssm_d[0], SSD_HEAD_DIM).reshape(1, D_INNER)
    y = _ssd_core(z, xbc, dt, ssm_conv_w[0], row(ssm_conv_b[0]), row(ssm_dt_bias[0]),
                  row(ssm_a_log[0]), d_exp, row(ssm_norm_w[0]), bsz, seq)
    zero_bias = jnp.zeros((1, D_MODEL), F32)
    x1, cls = _proj_ln_route(y, ssm_w_out[0].astype(BF16), zero_bias, xt,
                             row(ln_mix_g[0]), row(ln_mix_b[0]), rwt, rb)
    x2 = _moe_layer(x1, cls, wg[0], wu[0], wd[0], router_w, row(ln_ffn_g[0]), row(ln_ffn_b[0]))

    q, k, v = _qkv_proj(x2, att_w_qkv[0].astype(BF16), row(att_b_qkv[0]))
    bias = _attention_bias(att_rel_bias[0])
    att = _band_attention(q, k, v, bias, bsz, seq)
    x3, cls = _proj_ln_route(att, att_w_o[0].astype(BF16), row(att_b_o[0]), x2,
                             row(ln_mix_g[1]), row(ln_mix_b[1]), rwt, rb)
    x4 = _moe_layer(x3, cls, wg[1], wu[1], wd[1], router_w, row(ln_ffn_g[1]), row(ln_ffn_b[1]))
    return x4.reshape(bsz, seq, D_MODEL)
```

```python
import jax
import jax.numpy as jnp
from jax import lax
from jax.experimental import pallas as pl
from jax.experimental.pallas import tpu as pltpu

F32 = jnp.float32
BF16 = jnp.bfloat16
I32 = jnp.int32

D_MODEL = 1024
DEPTH = 2
CHUNK = 64
D_INNER = 2048
SSD_HEAD_DIM = 64
SSD_HEADS = 32
SSD_GROUPS = 4
SSD_HEADS_PER_GROUP = 8
SSD_STATE = 128
SSD_CONV = 4
SSD_BC = SSD_GROUPS * SSD_STATE
SSD_CONV_DIM = D_INNER + 2 * SSD_BC
ATT_HEADS = 16
ATT_HEAD_DIM = 64
LEFT_CHUNKS = 8
MAX_REL = 128
N_EXPERTS = 16
N_EXPERT_GROUPS = 4
EXPERTS_PER_GROUP = 4
D_FF = 512
ALPHA = (2.0 * DEPTH) ** 0.25
LN_EPS = 1e-5
RMS_EPS = 1e-5

PAIRS = ((0, 1), (0, 2), (0, 3), (1, 2), (1, 3), (2, 3))
N_CLASSES = N_EXPERT_GROUPS * len(PAIRS)
CLASS_ROWS = 32

LANES = 128
VMEM_LIMIT = 56 * 1024 * 1024

TM_PROJ = 512
SSD_BLOCK = 256
ATT_BLOCK = 128
ATT_KBLOCKS = 5
RANK_BLOCK = 1024
ROW_TILE = 256
NEG_BIG = -1e30


def _cparams(sem):
    return pltpu.CompilerParams(dimension_semantics=sem, vmem_limit_bytes=VMEM_LIMIT)


def _split3(a):
    hi = a.astype(BF16)
    r1 = a - hi.astype(F32)
    mid = r1.astype(BF16)
    lo = (r1 - mid.astype(F32)).astype(BF16)
    return hi, mid, lo


def _dot(a, b):
    return jnp.dot(a, b, preferred_element_type=F32)


def _dot_nt(a, b):
    return lax.dot_general(a, b, (((1,), (1,)), ((), ())), preferred_element_type=F32)


def _dot_tn(a, b):
    return lax.dot_general(a, b, (((0,), (0,)), ((), ())), preferred_element_type=F32)


def _sel_right(a, m01):
    hi, mid, lo = _split3(a)
    return _dot(hi, m01) + (_dot(mid, m01) + _dot(lo, m01))


def _sel_left(m01, a):
    hi, mid, lo = _split3(a)
    return _dot(m01, hi) + (_dot(m01, mid) + _dot(m01, lo))


def _sel_nt(m01, a):
    hi, mid, lo = _split3(a)
    return _dot_nt(m01, hi) + (_dot_nt(m01, mid) + _dot_nt(m01, lo))


def _dot_f32(a, b):
    ah, am, al = _split3(a)
    bh, bm, bl = _split3(b)
    return (_dot(ah, bh) + (_dot(ah, bm) + _dot(am, bh))
            + (_dot(am, bm) + _dot(ah, bl) + _dot(al, bh)))


def _dot_f32_nt(a, b):
    ah, am, al = _split3(a)
    bh, bm, bl = _split3(b)
    return (_dot_nt(ah, bh) + (_dot_nt(ah, bm) + _dot_nt(am, bh))
            + (_dot_nt(am, bm) + _dot_nt(ah, bl) + _dot_nt(al, bh)))


def _sigmoid(x):
    return 1.0 / (1.0 + jnp.exp(-x))


def _silu(x):
    return x * _sigmoid(x)


def _layer_norm(v, g, b):
    mu = jnp.mean(v, axis=-1, keepdims=True)
    c = v - mu
    var = jnp.mean(c * c, axis=-1, keepdims=True)
    return c * lax.rsqrt(var + LN_EPS) * g + b


def _in_proj_kernel(x_ref, wz_ref, wxbc_ref, wdt_ref, z_ref, xbc_ref, dt_ref):
    x = x_ref[...]
    xb = x.astype(BF16)
    nchunk = 512
    for j in range(D_INNER // nchunk):
        sl = slice(j * nchunk, (j + 1) * nchunk)
        z_ref[:, sl] = _dot(xb, wz_ref[:, sl]).astype(BF16)
    for j in range(SSD_CONV_DIM // nchunk):
        sl = slice(j * nchunk, (j + 1) * nchunk)
        xbc_ref[:, sl] = _dot(xb, wxbc_ref[:, sl]).astype(BF16)
    dt_ref[...] = _dot_f32(x, wdt_ref[...])


def _in_proj(x2d, wz, wxbc, wdt):
    t = x2d.shape[0]
    tm = min(TM_PROJ, t)
    const = lambda i: (0, 0)
    return pl.pallas_call(
        _in_proj_kernel,
        grid=(t // tm,),
        in_specs=[
            pl.BlockSpec((tm, D_MODEL), lambda i: (i, 0)),
            pl.BlockSpec((D_MODEL, D_INNER), const),
            pl.BlockSpec((D_MODEL, SSD_CONV_DIM), const),
            pl.BlockSpec((D_MODEL, SSD_HEADS), const),
        ],
        out_specs=[
            pl.BlockSpec((tm, D_INNER), lambda i: (i, 0)),
            pl.BlockSpec((tm, SSD_CONV_DIM), lambda i: (i, 0)),
            pl.BlockSpec((tm, SSD_HEADS), lambda i: (i, 0)),
        ],
        out_shape=[
            jax.ShapeDtypeStruct((t, D_INNER), BF16),
            jax.ShapeDtypeStruct((t, SSD_CONV_DIM), BF16),
            jax.ShapeDtypeStruct((t, SSD_HEADS), F32),
        ],
        compiler_params=_cparams(("parallel",)),
        name="ssd_in_proj",
    )(x2d, wz, wxbc, wdt)


def _ssd_kernel(z_ref, xbc_ref, dt_ref, cw_ref, cb_ref, dtb_ref, alog_ref, dexp_ref, nw_ref,
                y_ref, h_ref, carry_ref, ext_ref, xs_ref, b_ref, c_ref, yacc_ref):
    lb = z_ref.shape[0]
    nchunks = lb // CHUNK
    gp = SSD_HEADS_PER_GROUP * SSD_HEAD_DIM

    @pl.when(pl.program_id(1) == 0)
    def _():
        h_ref[...] = jnp.zeros_like(h_ref)
        carry_ref[...] = jnp.zeros_like(carry_ref)

    cchunk = 512
    for j in range(SSD_CONV_DIM // cchunk):
        sl = slice(j * cchunk, (j + 1) * cchunk)
        u = xbc_ref[:, sl].astype(F32)
        ext_ref[0:8, :] = carry_ref[:, sl]
        ext_ref[8:8 + lb, :] = u
        carry_ref[:, sl] = u[lb - 8:lb, :]
        acc = cb_ref[:, sl] + ext_ref[5:5 + lb, :] * cw_ref[0:1, sl]
        for k in range(1, SSD_CONV):
            acc = acc + ext_ref[5 + k:5 + k + lb, :] * cw_ref[k:k + 1, sl]
        act = _silu(acc)
        if j < D_INNER // cchunk:
            xs_ref[:, sl] = act
        elif j < (D_INNER + SSD_BC) // cchunk:
            b_ref[...] = act.astype(BF16)
        else:
            c_ref[...] = act.astype(BF16)

    dt_all = dt_ref[...] + dtb_ref[...]
    dt_all = jnp.maximum(dt_all, 0.0) + jnp.log1p(jnp.exp(-jnp.abs(dt_all)))
    a_neg = -jnp.exp(alog_ref[...])
    da_all = dt_all * a_neg

    r = lax.broadcasted_iota(I32, (CHUNK, CHUNK), 0)
    c = lax.broadcasted_iota(I32, (CHUNK, CHUNK), 1)
    causal = r >= c
    tril = causal.astype(BF16)
    eye_h = (lax.broadcasted_iota(I32, (SSD_HEADS, SSD_HEADS), 0)
             == lax.broadcasted_iota(I32, (SSD_HEADS, SSD_HEADS), 1)).astype(BF16)
    eh = lax.broadcasted_iota(I32, (SSD_HEADS, D_INNER), 0)
    ec = lax.broadcasted_iota(I32, (SSD_HEADS, D_INNER), 1)
    expand = ((ec // SSD_HEAD_DIM) == eh).astype(BF16)
    lane = lax.broadcasted_iota(I32, (CHUNK, LANES), 1)
    lo_half = lane < SSD_HEAD_DIM

    for ci in range(nchunks):
        rows = slice(ci * CHUNK, (ci + 1) * CHUNK)
        dt_c = dt_all[rows, :]
        da_c = da_all[rows, :]
        cum = _sel_left(tril, da_c)
        cum_t = _sel_nt(eye_h, cum)
        dt_t = _sel_nt(eye_h, dt_c)
        last = cum[CHUNK - 1:CHUNK, :]
        ecum_x = _sel_right(jnp.exp(cum), expand)
        wj_x = _sel_right(jnp.exp(last - cum) * dt_c, expand)
        elast_x = _sel_right(jnp.exp(last), expand)

        xs_c = xs_ref[rows, :]
        xs_b = xs_c.astype(BF16)
        xw_b = (xs_c * wj_x).astype(BF16)
        b_c = b_ref[rows, :]
        c_c = c_ref[rows, :]

        for g in range(SSD_GROUPS):
            ns = slice(g * SSD_STATE, (g + 1) * SSD_STATE)
            gs = slice(g * gp, (g + 1) * gp)
            cb = _dot_nt(c_c[:, ns], b_c[:, ns])
            h_g = h_ref[g]
            y_inter = _dot(c_c[:, ns], h_g.astype(BF16)) * ecum_x[:, gs]
            for hp in range(SSD_HEADS_PER_GROUP // 2):
                col = g * gp + hp * LANES
                xpair = xs_b[:, col:col + LANES]
                y_pair = jnp.zeros((CHUNK, LANES), F32)
                for sub in range(2):
                    h = g * SSD_HEADS_PER_GROUP + hp * 2 + sub
                    seg = cum[:, h:h + 1] - cum_t[h:h + 1, :]
                    decay = jnp.exp(jnp.where(causal, seg, -jnp.inf))
                    w = (cb * decay * dt_t[h:h + 1, :]).astype(BF16)
                    keep = lo_half if sub == 0 else jnp.logical_not(lo_half)
                    xh = jnp.where(keep, xpair, jnp.zeros_like(xpair))
                    y_pair = y_pair + _dot(w, xh)
                yacc_ref[rows, col:col + LANES] = (
                    y_pair + y_inter[:, hp * LANES:(hp + 1) * LANES])
            h_ref[g] = h_g * elast_x[:, gs] + _dot_tn(b_c[:, ns], xw_b[:, gs])

    for g in range(SSD_GROUPS):
        gs = slice(g * gp, (g + 1) * gp)
        y = yacc_ref[:, gs] + dexp_ref[:, gs] * xs_ref[:, gs]
        y = y * _silu(z_ref[:, gs].astype(F32))
        ms = jnp.mean(y * y, axis=-1, keepdims=True)
        y_ref[:, gs] = (y * lax.rsqrt(ms + RMS_EPS) * nw_ref[:, gs]).astype(BF16)


def _ssd_core(z, xbc, dt, conv_w, conv_b, dt_bias, a_log, d_exp, norm_w, bsz, seq):
    lb = min(SSD_BLOCK, seq)
    nblk = seq // lb
    row = lambda b, c: (b * nblk + c, 0)
    const = lambda b, c: (0, 0)
    return pl.pallas_call(
        _ssd_kernel,
        grid=(bsz, nblk),
        in_specs=[
            pl.BlockSpec((lb, D_INNER), row),
            pl.BlockSpec((lb, SSD_CONV_DIM), row),
            pl.BlockSpec((lb, SSD_HEADS), row),
            pl.BlockSpec((SSD_CONV, SSD_CONV_DIM), const),
            pl.BlockSpec((1, SSD_CONV_DIM), const),
            pl.BlockSpec((1, SSD_HEADS), const),
            pl.BlockSpec((1, SSD_HEADS), const),
            pl.BlockSpec((1, D_INNER), const),
            pl.BlockSpec((1, D_INNER), const),
        ],
        out_specs=pl.BlockSpec((lb, D_INNER), row),
        out_shape=jax.ShapeDtypeStruct((bsz * seq, D_INNER), BF16),
        scratch_shapes=[
            pltpu.VMEM((SSD_GROUPS, SSD_STATE, SSD_HEADS_PER_GROUP * SSD_HEAD_DIM), F32),
            pltpu.VMEM((8, SSD_CONV_DIM), F32),
            pltpu.VMEM((lb + 8, 512), F32),
            pltpu.VMEM((lb, D_INNER), F32),
            pltpu.VMEM((lb, SSD_BC), BF16),
            pltpu.VMEM((lb, SSD_BC), BF16),
            pltpu.VMEM((lb, D_INNER), F32),
        ],
        compiler_params=_cparams(("parallel", "arbitrary")),
        name="ssd_core",
    )(z, xbc, dt, conv_w, conv_b, dt_bias, a_log, d_exp, norm_w)


def _route(x, rwt_ref, rb_ref):
    logits = _dot_f32_nt(rwt_ref[...], x)
    m = jnp.max(logits, axis=0, keepdims=True)
    e = jnp.exp(logits - m)
    scores = e / jnp.sum(e, axis=0, keepdims=True)
    sel = scores + rb_ref[...]
    s = [sel[i:i + 1, :] for i in range(N_EXPERTS)]
    gsum = []
    for g in range(N_EXPERT_GROUPS):
        v = s[g * EXPERTS_PER_GROUP:(g + 1) * EXPERTS_PER_GROUP]
        best = v[0] + v[1]
        for (i, j) in PAIRS[1:]:
            best = jnp.maximum(best, v[i] + v[j])
        gsum.append(best)
    grp = jnp.zeros_like(gsum[0], dtype=I32)
    best = gsum[0]
    for g in range(1, N_EXPERT_GROUPS):
        better = gsum[g] > best
        grp = jnp.where(better, g, grp)
        best = jnp.where(better, gsum[g], best)
    v = []
    for k in range(EXPERTS_PER_GROUP):
        vk = s[k]
        for g in range(1, N_EXPERT_GROUPS):
            vk = jnp.where(grp == g, s[g * EXPERTS_PER_GROUP + k], vk)
        v.append(vk)
    i1 = jnp.zeros_like(grp)
    best = v[0]
    for k in range(1, EXPERTS_PER_GROUP):
        better = v[k] > best
        i1 = jnp.where(better, k, i1)
        best = jnp.where(better, v[k], best)
    i2 = jnp.full_like(grp, -1)
    best = jnp.full_like(v[0], -jnp.inf)
    for k in range(EXPERTS_PER_GROUP):
        better = jnp.logical_and(i1 != k, jnp.logical_or(v[k] > best, i2 < 0))
        i2 = jnp.where(better, k, i2)
        best = jnp.where(better, v[k], best)
    lo = jnp.minimum(i1, i2)
    hi = jnp.maximum(i1, i2)
    pair = jnp.where(lo == 0, hi - 1, jnp.where(lo == 1, hi + 1, 5))
    return grp * len(PAIRS) + pair


def _proj_ln_route_kernel(a_ref, w_ref, bias_ref, xres_ref, g_ref, b_ref, rwt_ref, rb_ref,
                          xo_ref, cls_ref):
    acc = _dot(a_ref[...], w_ref[...]) + bias_ref[...]
    v = ALPHA * xres_ref[...] + acc
    out = _layer_norm(v, g_ref[...], b_ref[...])
    xo_ref[...] = out
    cls_ref[...] = _route(out, rwt_ref, rb_ref)


def _proj_ln_route(a, w, bias, xres, g, b, rwt, rb):
    t, k = a.shape
    tm = min(TM_PROJ, t)
    const = lambda i: (0, 0)
    return pl.pallas_call(
        _proj_ln_route_kernel,
        grid=(t // tm,),
        in_specs=[
            pl.BlockSpec((tm, k), lambda i: (i, 0)),
            pl.BlockSpec((k, D_MODEL), const),
            pl.BlockSpec((1, D_MODEL), const),
            pl.BlockSpec((tm, D_MODEL), lambda i: (i, 0)),
            pl.BlockSpec((1, D_MODEL), const),
            pl.BlockSpec((1, D_MODEL), const),
            pl.BlockSpec((N_EXPERTS, D_MODEL), const),
            pl.BlockSpec((N_EXPERTS, 1), const),
        ],
        out_specs=[
            pl.BlockSpec((tm, D_MODEL), lambda i: (i, 0)),
            pl.BlockSpec((1, tm), lambda i: (0, i)),
        ],
        out_shape=[
            jax.ShapeDtypeStruct((t, D_MODEL), F32),
            jax.ShapeDtypeStruct((1, t), I32),
        ],
        compiler_params=_cparams(("parallel",)),
        name="proj_ln_route",
    )(a, w, bias, xres, g, b, rwt, rb)


def _positions_kernel(cls_ref, pos_ref, cnt_ref, run_ref, offs_ref, upper_ref):
    tb = cls_ref.shape[1]
    phase = pl.program_id(0)
    blk = pl.program_id(1)

    @pl.when(jnp.logical_and(phase == 0, blk == 0))
    def _():
        run_ref[...] = jnp.zeros_like(run_ref)
        r = lax.broadcasted_iota(I32, (tb, tb), 0)
        c = lax.broadcasted_iota(I32, (tb, tb), 1)
        upper_ref[...] = (r < c).astype(BF16)

    @pl.when(jnp.logical_and(phase == 1, blk == 0))
    def _():
        counts = run_ref[...]
        cnt_ref[...] = counts.astype(I32)
        padded = jnp.ceil(counts * (1.0 / ROW_TILE)) * ROW_TILE
        r = lax.broadcasted_iota(I32, (CLASS_ROWS, CLASS_ROWS), 0)
        c = lax.broadcasted_iota(I32, (CLASS_ROWS, CLASS_ROWS), 1)
        strict_lower = (c < r).astype(BF16)
        wide = jnp.broadcast_to(padded, (CLASS_ROWS, LANES))
        offs_ref[...] = _sel_left(strict_lower, wide)[:, 0:1]
        run_ref[...] = jnp.zeros_like(run_ref)

    cls = cls_ref[...]
    cid = lax.broadcasted_iota(I32, (CLASS_ROWS, tb), 0)
    onehot = cid == cls
    run = run_ref[...]

    @pl.when(phase == 1)
    def _():
        before = _dot(onehot.astype(BF16), upper_ref[...])
        pos = jnp.sum(jnp.where(onehot, before + (run + offs_ref[...]), 0.0), axis=0, keepdims=True)
        pos_ref[...] = pos.astype(I32)

    run_ref[...] = run + jnp.sum(onehot.astype(F32), axis=1, keepdims=True)


def _sorted_positions(cls):
    t = cls.shape[1]
    tb = min(RANK_BLOCK, t)
    return pl.pallas_call(
        _positions_kernel,
        grid=(2, t // tb),
        in_specs=[pl.BlockSpec((1, tb), lambda p, i: (0, i))],
        out_specs=[
            pl.BlockSpec((1, tb), lambda p, i: (0, i * p)),
            pl.BlockSpec((CLASS_ROWS, 1), lambda p, i: (0, 0)),
        ],
        out_shape=[
            jax.ShapeDtypeStruct((1, t), I32),
            jax.ShapeDtypeStruct((CLASS_ROWS, 1), I32),
        ],
        scratch_shapes=[
            pltpu.VMEM((CLASS_ROWS, 1), F32),
            pltpu.VMEM((CLASS_ROWS, 1), F32),
            pltpu.VMEM((tb, tb), BF16),
        ],
        compiler_params=_cparams(("arbitrary", "arbitrary")),
        name="sorted_positions",
    )(cls)


def _invert_kernel(pos_ref, tok_ref):
    def clear(r, carry):
        tok_ref[r] = 0
        return carry

    lax.fori_loop(0, tok_ref.shape[0], clear, 0, unroll=8)

    def place(t, carry):
        tok_ref[pos_ref[t]] = t
        return carry

    lax.fori_loop(0, pos_ref.shape[0], place, 0, unroll=8)


def _invert_positions(pos, n_rows):
    return pl.pallas_call(
        _invert_kernel,
        in_specs=[pl.BlockSpec(memory_space=pltpu.SMEM)],
        out_specs=pl.BlockSpec(memory_space=pltpu.SMEM),
        out_shape=jax.ShapeDtypeStruct((n_rows,), I32),
        name="invert_positions",
    )(pos)


def _row_gather(x_hbm, xbuf, sems, tok_ref, tile, slot):
    def copy(r):
        tok = tok_ref[tile * ROW_TILE + r]
        return pltpu.make_async_copy(x_hbm.at[pl.ds(tok, 1), :],
                                     xbuf.at[slot, pl.ds(r, 1), :], sems.at[slot])
    return copy


def _row_scatter(obuf, out_hbm, sems, tok_ref, tile, slot):
    def copy(r):
        tok = tok_ref[tile * ROW_TILE + r]
        return pltpu.make_async_copy(obuf.at[slot, pl.ds(r, 1), :],
                                     out_hbm.at[pl.ds(tok, 1), :], sems.at[slot])
    return copy


def _start_rows(copy, n):
    def body(r, carry):
        copy(r).start()
        return carry
    lax.fori_loop(0, n, body, 0)


def _wait_rows(hbm, buf, sem, slot, n, hbm_is_src):
    n8 = pl.multiple_of((n // 8) * 8, 8)

    def desc(rows):
        h, v = hbm.at[rows, :], buf.at[slot, rows, :]
        return pltpu.make_async_copy(h, v, sem.at[slot]) if hbm_is_src else \
            pltpu.make_async_copy(v, h, sem.at[slot])

    @pl.when(n8 > 0)
    def _():
        desc(pl.ds(0, n8)).wait()

    def one(r, carry):
        desc(pl.ds(0, 1)).wait()
        return carry
    lax.fori_loop(0, n - n8, one, 0)


def _ffn_kernel(e1_ref, e2_ref, nrows_ref, tok_ref, x_hbm, wg1_ref, wu1_ref, wd1_ref,
                wg2_ref, wu2_ref, wd2_ref, rw_ref, g_ref, b_ref, out_hbm,
                xbuf, obuf, sem_in, sem_out):
    i = pl.program_id(0)
    nt = pl.num_programs(0)
    slot = i % 2
    n_i = nrows_ref[i]
    up8 = lambda n: ((n + 7) // 8) * 8

    @pl.when(i == 0)
    def _():
        xbuf[...] = jnp.zeros_like(xbuf)
        _start_rows(_row_gather(x_hbm, xbuf, sem_in, tok_ref, 0, 0), up8(nrows_ref[0]))

    @pl.when(i + 1 < nt)
    def _():
        _start_rows(_row_gather(x_hbm, xbuf, sem_in, tok_ref, i + 1, 1 - slot),
                    up8(nrows_ref[i + 1]))

    _wait_rows(x_hbm, xbuf, sem_in, slot, up8(n_i), True)

    @pl.when(i >= 2)
    def _():
        _wait_rows(out_hbm, obuf, sem_out, slot, nrows_ref[i - 2], False)

    @pl.when(n_i > 0)
    def _():
        x = xbuf[slot]
        xb = x.astype(BF16)
        logits = _dot_f32(x, rw_ref[...])
        m = jnp.max(logits, axis=-1, keepdims=True)
        e = jnp.exp(logits - m)
        scores = e / jnp.sum(e, axis=-1, keepdims=True)
        lane = lax.broadcasted_iota(I32, scores.shape, 1)
        s1 = jnp.sum(jnp.where(lane == e1_ref[i], scores, 0.0), axis=-1, keepdims=True)
        s2 = jnp.sum(jnp.where(lane == e2_ref[i], scores, 0.0), axis=-1, keepdims=True)
        den = s1 + s2
        h1 = _silu(_dot(xb, wg1_ref[...])) * _dot(xb, wu1_ref[...])
        h2 = _silu(_dot(xb, wg2_ref[...])) * _dot(xb, wu2_ref[...])
        ffn = ((s1 / den) * _dot(h1.astype(BF16), wd1_ref[...])
               + (s2 / den) * _dot(h2.astype(BF16), wd2_ref[...]))
        obuf[slot] = _layer_norm(ALPHA * x + ffn, g_ref[...], b_ref[...])
        _start_rows(_row_scatter(obuf, out_hbm, sem_out, tok_ref, i, slot), n_i)

    @pl.when(i == nt - 1)
    def _():
        @pl.when(i >= 1)
        def _():
            _wait_rows(out_hbm, obuf, sem_out, 1 - slot, nrows_ref[i - 1], False)
        _wait_rows(out_hbm, obuf, sem_out, slot, n_i, False)


def _grouped_ffn(x, tok, tile_e1, tile_e2, tile_rows, wg, wu, wd, rw, g, b):
    t = x.shape[0]
    rt = ROW_TILE
    nt = tile_rows.shape[0]
    const = lambda i, *_: (0, 0)
    w1 = lambda i, e1, e2, n, tk: (e1[i], 0, 0)
    w2 = lambda i, e1, e2, n, tk: (e2[i], 0, 0)
    grid_spec = pltpu.PrefetchScalarGridSpec(
        num_scalar_prefetch=4,
        grid=(nt,),
        in_specs=[
            pl.BlockSpec(memory_space=pl.ANY),
            pl.BlockSpec((None, D_MODEL, D_FF), w1),
            pl.BlockSpec((None, D_MODEL, D_FF), w1),
            pl.BlockSpec((None, D_FF, D_MODEL), w1),
            pl.BlockSpec((None, D_MODEL, D_FF), w2),
            pl.BlockSpec((None, D_MODEL, D_FF), w2),
            pl.BlockSpec((None, D_FF, D_MODEL), w2),
            pl.BlockSpec((D_MODEL, N_EXPERTS), const),
            pl.BlockSpec((1, D_MODEL), const),
            pl.BlockSpec((1, D_MODEL), const),
        ],
        out_specs=pl.BlockSpec(memory_space=pl.ANY),
        scratch_shapes=[
            pltpu.VMEM((2, rt, D_MODEL), F32),
            pltpu.VMEM((2, rt, D_MODEL), F32),
            pltpu.SemaphoreType.DMA((2,)),
            pltpu.SemaphoreType.DMA((2,)),
        ],
    )
    return pl.pallas_call(
        _ffn_kernel,
        grid_spec=grid_spec,
        out_shape=jax.ShapeDtypeStruct((t, D_MODEL), F32),
        compiler_params=pltpu.CompilerParams(dimension_semantics=("arbitrary",),
                                             vmem_limit_bytes=VMEM_LIMIT, has_side_effects=True),
        name="grouped_ffn",
    )(tile_e1, tile_e2, tile_rows, tok, x, wg, wu, wd, wg, wu, wd, rw, g, b)


def _tile_table(counts, nt):
    rt = ROW_TILE
    padded = ((counts + rt - 1) // rt) * rt
    ends = jnp.cumsum(padded)
    filled = ends - padded + counts
    start = jnp.arange(nt, dtype=I32) * rt
    tile_cls = jnp.sum((start[:, None] >= ends[None, :]).astype(I32), axis=1)
    onehot = tile_cls[:, None] == jnp.arange(N_CLASSES, dtype=I32)[None, :]
    tile_fill = jnp.sum(jnp.where(onehot, filled[None, :], 0), axis=1)
    tile_rows = jnp.clip(tile_fill - start, 0, rt).astype(I32)
    last_cls = jnp.max(jnp.where(tile_rows > 0, tile_cls, 0))
    tile_cls = jnp.where(tile_rows > 0, tile_cls, last_cls)
    grp = tile_cls // len(PAIRS)
    pair = tile_cls % len(PAIRS)
    ge3 = (pair >= 3).astype(I32)
    ge5 = (pair >= 5).astype(I32)
    lo = ge3 + ge5
    hi = pair + 1 - 2 * ge3 - ge5
    return grp * EXPERTS_PER_GROUP + lo, grp * EXPERTS_PER_GROUP + hi, tile_rows


def _moe_layer(x, cls, wg, wu, wd, rw, g, b):
    t = x.shape[0]
    nt = t // ROW_TILE + N_CLASSES
    pos, counts = _sorted_positions(cls)
    tok = _invert_positions(pos.reshape(t), nt * ROW_TILE)
    tile_e1, tile_e2, tile_rows = _tile_table(counts[:N_CLASSES, 0], nt)
    return _grouped_ffn(x, tok, tile_e1, tile_e2, tile_rows, wg, wu, wd, rw, g, b)


def _qkv_kernel(x_ref, w_ref, b_ref, q_ref, k_ref, v_ref):
    xb = x_ref[...].astype(BF16)
    nchunk = 512
    for o, ref in enumerate((q_ref, k_ref, v_ref)):
        for j in range(D_MODEL // nchunk):
            sl = slice(j * nchunk, (j + 1) * nchunk)
            wsl = slice(o * D_MODEL + j * nchunk, o * D_MODEL + (j + 1) * nchunk)
            r = _dot(xb, w_ref[:, wsl]) + b_ref[:, wsl]
            if o == 0:
                r = r * (ATT_HEAD_DIM ** -0.5)
            ref[:, sl] = r.astype(BF16)


def _qkv_proj(x2d, w, b):
    t = x2d.shape[0]
    tm = min(TM_PROJ, t)
    const = lambda i: (0, 0)
    row = pl.BlockSpec((tm, D_MODEL), lambda i: (i, 0))
    return pl.pallas_call(
        _qkv_kernel,
        grid=(t // tm,),
        in_specs=[row, pl.BlockSpec((D_MODEL, 3 * D_MODEL), const),
                  pl.BlockSpec((1, 3 * D_MODEL), const)],
        out_specs=[row, row, row],
        out_shape=[jax.ShapeDtypeStruct((t, D_MODEL), BF16)] * 3,
        compiler_params=_cparams(("parallel",)),
        name="qkv_proj",
    )(x2d, w, b)


def _attn_kernel(q_ref, *rest):
    k_refs = rest[:ATT_KBLOCKS]
    v_refs = rest[ATT_KBLOCKS:2 * ATT_KBLOCKS]
    bias_ref = rest[2 * ATT_KBLOCKS]
    o_ref = rest[2 * ATT_KBLOCKS + 1]
    qi = pl.program_id(1)
    nkeys = ATT_KBLOCKS * ATT_BLOCK
    first_valid = jnp.maximum((ATT_KBLOCKS - 1 - qi) * ATT_BLOCK, 0)
    col = lax.broadcasted_iota(I32, (ATT_BLOCK, nkeys), 1)
    key_ok = col >= first_valid
    lane = lax.broadcasted_iota(I32, (ATT_BLOCK, LANES), 1)
    lo_half = lane < ATT_HEAD_DIM
    for hp in range(ATT_HEADS // 2):
        cs = slice(hp * LANES, (hp + 1) * LANES)
        qp = q_ref[:, cs]
        out_pair = jnp.zeros((ATT_BLOCK, LANES), F32)
        for sub in range(2):
            keep = lo_half if sub == 0 else jnp.logical_not(lo_half)
            qm = jnp.where(keep, qp, jnp.zeros_like(qp))
            s = jnp.concatenate([_dot_nt(qm, k_refs[d][:, cs]) for d in range(ATT_KBLOCKS)],
                                axis=1)
            s = s + bias_ref[hp * 2 + sub]
            s = jnp.where(key_ok, s, NEG_BIG)
            m = jnp.max(s, axis=-1, keepdims=True)
            p = jnp.exp(s - m)
            l = jnp.sum(p, axis=-1, keepdims=True)
            pb = p.astype(BF16)
            o = _dot(pb[:, 0:ATT_BLOCK], v_refs[0][:, cs])
            for d in range(1, ATT_KBLOCKS):
                o = o + _dot(pb[:, d * ATT_BLOCK:(d + 1) * ATT_BLOCK], v_refs[d][:, cs])
            out_pair = jnp.where(keep, o / l, out_pair)
        o_ref[:, cs] = out_pair.astype(BF16)


def _band_attention(q, k, v, bias, bsz, seq):
    nq = seq // ATT_BLOCK
    row = lambda b, i: (b * nq + i, 0)

    def kv_spec(d):
        shift = ATT_KBLOCKS - 1 - d
        return pl.BlockSpec((ATT_BLOCK, D_MODEL),
                            lambda b, i: (b * nq + jnp.maximum(i - shift, 0), 0))

    kspecs = [kv_spec(d) for d in range(ATT_KBLOCKS)]
    return pl.pallas_call(
        _attn_kernel,
        grid=(bsz, nq),
        in_specs=[pl.BlockSpec((ATT_BLOCK, D_MODEL), row)] + kspecs + kspecs + [
            pl.BlockSpec((ATT_HEADS, ATT_BLOCK, ATT_KBLOCKS * ATT_BLOCK), lambda b, i: (0, 0, 0))],
        out_specs=pl.BlockSpec((ATT_BLOCK, D_MODEL), row),
        out_shape=jax.ShapeDtypeStruct((bsz * seq, D_MODEL), BF16),
        compiler_params=_cparams(("parallel", "arbitrary")),
        name="band_attention",
    )(q, *([k] * ATT_KBLOCKS), *([v] * ATT_KBLOCKS), bias)


def _attention_bias(rel_table):
    nkeys = ATT_KBLOCKS * ATT_BLOCK
    r = jnp.arange(ATT_BLOCK)[:, None]
    j = jnp.arange(nkeys)[None, :]
    dist = r + LEFT_CHUNKS * CHUNK - j
    idx = jnp.clip(dist, -MAX_REL, MAX_REL) + MAX_REL
    onehot = (idx[:, :, None] == jnp.arange(2 * MAX_REL + 1)[None, None, :]).astype(F32)
    bias = jnp.einsum("rjk,kh->hrj", onehot, rel_table.astype(F32),
                      precision=lax.Precision.HIGHEST)
    first = r // CHUNK
    in_band = jnp.logical_and(j >= first * CHUNK, j < first * CHUNK + (LEFT_CHUNKS + 1) * CHUNK)
    return jnp.where(in_band[None], bias, NEG_BIG)


def kernel(x, ssm_w_in, ssm_conv_w, ssm_conv_b, ssm_dt_bias, ssm_a_log, ssm_d, ssm_norm_w, ssm_w_out, att_w_qkv, att_b_qkv, att_rel_bias, att_w_o, att_b_o, router_w, router_bias, moe_w_gate, moe_w_up, moe_w_down, ln_mix_g, ln_mix_b, ln_ffn_g, ln_ffn_b):
    bsz, seq, _ = x.shape
    t = bsz * seq
    xt = x.reshape(t, D_MODEL)
    rwt = router_w.T
    rb = router_bias.reshape(N_EXPERTS, 1)
    row = lambda a: a.reshape(1, -1)
    wg = moe_w_gate.astype(BF16)
    wu = moe_w_up.astype(BF16)
    wd = moe_w_down.astype(BF16)

    w_in = ssm_w_in[0]
    wz = w_in[:, :D_INNER].astype(BF16)
    wxbc = w_in[:, D_INNER:D_INNER + SSD_CONV_DIM].astype(BF16)
    wdt = w_in[:, D_INNER + SSD_CONV_DIM:]
    z, xbc, dt = _in_proj(xt, wz, wxbc, wdt)
    d_exp = jnp.repeat(ssm_d[0], SSD_HEAD_DIM).reshape(1, D_INNER)
    y = _ssd_core(z, xbc, dt, ssm_conv_w[0], row(ssm_conv_b[0]), row(ssm_dt_bias[0]),
                  row(ssm_a_log[0]), d_exp, row(ssm_norm_w[0]), bsz, seq)
    zero_bias = jnp.zeros((1, D_MODEL), F32)
    x1, cls = _proj_ln_route(y, ssm_w_out[0].astype(BF16), zero_bias, xt,
                             row(ln_mix_g[0]), row(ln_mix_b[0]), rwt, rb)
    x2 = _moe_layer(x1, cls, wg[0], wu[0], wd[0], router_w, row(ln_ffn_g[0]), row(ln_ffn_b[0]))

    q, k, v = _qkv_proj(x2, att_w_qkv[0].astype(BF16), row(att_b_qkv[0]))
    bias = _attention_bias(att_rel_bias[0])
    att = _band_attention(q, k, v, bias, bsz, seq)
    x3, cls = _proj_ln_route(att, att_w_o[0].astype(BF16), row(att_b_o[0]), x2,
                             row(ln_mix_g[1]), row(ln_mix_b[1]), rwt, rb)
    x4 = _moe_layer(x3, cls, wg[1], wu[1], wd[1], router_w, row(ln_ffn_g[1]), row(ln_ffn_b[1]))
    return x4.reshape(bsz, seq, D_MODEL)
```

```python
import jax
import jax.numpy as jnp
from jax import lax
from jax.experimental import pallas as pl
from jax.experimental.pallas import tpu as pltpu

F32 = jnp.float32
BF16 = jnp.bfloat16
I32 = jnp.int32

D_MODEL = 1024
DEPTH = 2
CHUNK = 64
D_INNER = 2048
SSD_HEAD_DIM = 64
SSD_HEADS = 32
SSD_GROUPS = 4
SSD_HEADS_PER_GROUP = 8
SSD_STATE = 128
SSD_CONV = 4
SSD_BC = SSD_GROUPS * SSD_STATE
SSD_CONV_DIM = D_INNER + 2 * SSD_BC
ATT_HEADS = 16
ATT_HEAD_DIM = 64
LEFT_CHUNKS = 8
MAX_REL = 128
N_EXPERTS = 16
N_EXPERT_GROUPS = 4
EXPERTS_PER_GROUP = 4
D_FF = 512
ALPHA = (2.0 * DEPTH) ** 0.25
LN_EPS = 1e-5
RMS_EPS = 1e-5

PAIRS = ((0, 1), (0, 2), (0, 3), (1, 2), (1, 3), (2, 3))
N_CLASSES = N_EXPERT_GROUPS * len(PAIRS)
CLASS_ROWS = 32

LANES = 128
ROW_SUB = D_MODEL // LANES
VMEM_LIMIT = 56 * 1024 * 1024

TM_PROJ = 512
SSD_BLOCK = 256
ATT_BLOCK = 128
ATT_KBLOCKS = 5
ATT_HEAD_GROUP = 8
RANK_BLOCK = 1024
ROW_TILE = 256
NEG_BIG = -1e30


def _cparams(sem):
    return pltpu.CompilerParams(dimension_semantics=sem, vmem_limit_bytes=VMEM_LIMIT)


def _split3(a):
    hi = a.astype(BF16)
    r1 = a - hi.astype(F32)
    mid = r1.astype(BF16)
    lo = (r1 - mid.astype(F32)).astype(BF16)
    return hi, mid, lo


def _dot(a, b):
    return jnp.dot(a, b, preferred_element_type=F32)


def _dot_nt(a, b):
    return lax.dot_general(a, b, (((1,), (1,)), ((), ())), preferred_element_type=F32)


def _dot_tn(a, b):
    return lax.dot_general(a, b, (((0,), (0,)), ((), ())), preferred_element_type=F32)


def _sel_right(a, m01):
    hi, mid, lo = _split3(a)
    return _dot(hi, m01) + (_dot(mid, m01) + _dot(lo, m01))


def _sel_left(m01, a):
    hi, mid, lo = _split3(a)
    return _dot(m01, hi) + (_dot(m01, mid) + _dot(m01, lo))


def _sel_nt(m01, a):
    hi, mid, lo = _split3(a)
    return _dot_nt(m01, hi) + (_dot_nt(m01, mid) + _dot_nt(m01, lo))


def _dot_f32(a, b):
    ah, am, al = _split3(a)
    bh, bm, bl = _split3(b)
    return (_dot(ah, bh) + (_dot(ah, bm) + _dot(am, bh))
            + (_dot(am, bm) + _dot(ah, bl) + _dot(al, bh)))


def _dot_f32_nt(a, b):
    ah, am, al = _split3(a)
    bh, bm, bl = _split3(b)
    return (_dot_nt(ah, bh) + (_dot_nt(ah, bm) + _dot_nt(am, bh))
            + (_dot_nt(am, bm) + _dot_nt(ah, bl) + _dot_nt(al, bh)))


def _rows_load(ref, lead=()):
    n = ref.shape[-2] // ROW_SUB
    return jnp.concatenate([ref[lead + (pl.ds(s, n, stride=ROW_SUB), slice(None))]
                            for s in range(ROW_SUB)], axis=1)


def _rows_store(ref, v, lead=()):
    n = ref.shape[-2] // ROW_SUB
    for s in range(ROW_SUB):
        ref[lead + (pl.ds(s, n, stride=ROW_SUB), slice(None))] = v[:, s * LANES:(s + 1) * LANES]


def _sigmoid(x):
    return 1.0 / (1.0 + jnp.exp(-x))


def _silu(x):
    return x * _sigmoid(x)


def _layer_norm(v, g, b):
    mu = jnp.mean(v, axis=-1, keepdims=True)
    c = v - mu
    var = jnp.mean(c * c, axis=-1, keepdims=True)
    return c * lax.rsqrt(var + LN_EPS) * g + b


def _in_proj_kernel(x_ref, wz_ref, wxbc_ref, wdt_ref, z_ref, xbc_ref, dt_ref):
    x = x_ref[...]
    xb = x.astype(BF16)
    nchunk = 512
    for j in range(D_INNER // nchunk):
        sl = slice(j * nchunk, (j + 1) * nchunk)
        z_ref[:, sl] = _dot(xb, wz_ref[:, sl]).astype(BF16)
    for j in range(SSD_CONV_DIM // nchunk):
        sl = slice(j * nchunk, (j + 1) * nchunk)
        xbc_ref[:, sl] = _dot(xb, wxbc_ref[:, sl]).astype(BF16)
    dt_ref[...] = _dot_f32(x, wdt_ref[...])


def _in_proj(x2d, wz, wxbc, wdt):
    t = x2d.shape[0]
    tm = min(TM_PROJ, t)
    const = lambda i: (0, 0)
    return pl.pallas_call(
        _in_proj_kernel,
        grid=(t // tm,),
        in_specs=[
            pl.BlockSpec((tm, D_MODEL), lambda i: (i, 0)),
            pl.BlockSpec((D_MODEL, D_INNER), const),
            pl.BlockSpec((D_MODEL, SSD_CONV_DIM), const),
            pl.BlockSpec((D_MODEL, SSD_HEADS), const),
        ],
        out_specs=[
            pl.BlockSpec((tm, D_INNER), lambda i: (i, 0)),
            pl.BlockSpec((tm, SSD_CONV_DIM), lambda i: (i, 0)),
            pl.BlockSpec((tm, SSD_HEADS), lambda i: (i, 0)),
        ],
        out_shape=[
            jax.ShapeDtypeStruct((t, D_INNER), BF16),
            jax.ShapeDtypeStruct((t, SSD_CONV_DIM), BF16),
            jax.ShapeDtypeStruct((t, SSD_HEADS), F32),
        ],
        compiler_params=_cparams(("parallel",)),
        name="ssd_in_proj",
    )(x2d, wz, wxbc, wdt)


def _ssd_kernel(z_ref, xbc_ref, dt_ref, cw_ref, cb_ref, dtb_ref, alog_ref, dexp_ref, nw_ref,
                y_ref, h_ref, carry_ref, ext_ref, xs_ref, b_ref, c_ref, yacc_ref):
    lb = z_ref.shape[0]
    nchunks = lb // CHUNK
    gp = SSD_HEADS_PER_GROUP * SSD_HEAD_DIM

    @pl.when(pl.program_id(1) == 0)
    def _():
        h_ref[...] = jnp.zeros_like(h_ref)
        carry_ref[...] = jnp.zeros_like(carry_ref)

    cchunk = 512
    for j in range(SSD_CONV_DIM // cchunk):
        sl = slice(j * cchunk, (j + 1) * cchunk)
        u = xbc_ref[:, sl].astype(F32)
        ext_ref[0:8, :] = carry_ref[:, sl]
        ext_ref[8:8 + lb, :] = u
        carry_ref[:, sl] = u[lb - 8:lb, :]
        acc = cb_ref[:, sl] + ext_ref[5:5 + lb, :] * cw_ref[0:1, sl]
        for k in range(1, SSD_CONV):
            acc = acc + ext_ref[5 + k:5 + k + lb, :] * cw_ref[k:k + 1, sl]
        act = _silu(acc)
        if j < D_INNER // cchunk:
            xs_ref[:, sl] = act
        elif j < (D_INNER + SSD_BC) // cchunk:
            b_ref[...] = act.astype(BF16)
        else:
            c_ref[...] = act.astype(BF16)

    dt_all = dt_ref[...] + dtb_ref[...]
    dt_all = jnp.maximum(dt_all, 0.0) + jnp.log1p(jnp.exp(-jnp.abs(dt_all)))
    a_neg = -jnp.exp(alog_ref[...])
    da_all = dt_all * a_neg

    r = lax.broadcasted_iota(I32, (CHUNK, CHUNK), 0)
    c = lax.broadcasted_iota(I32, (CHUNK, CHUNK), 1)
    causal = r >= c
    tril = causal.astype(BF16)
    eye_h = (lax.broadcasted_iota(I32, (SSD_HEADS, SSD_HEADS), 0)
             == lax.broadcasted_iota(I32, (SSD_HEADS, SSD_HEADS), 1)).astype(BF16)
    eh = lax.broadcasted_iota(I32, (SSD_HEADS, D_INNER), 0)
    ec = lax.broadcasted_iota(I32, (SSD_HEADS, D_INNER), 1)
    expand = ((ec // SSD_HEAD_DIM) == eh).astype(BF16)
    lane = lax.broadcasted_iota(I32, (CHUNK, LANES), 1)
    lo_half = lane < SSD_HEAD_DIM

    for ci in range(nchunks):
        rows = slice(ci * CHUNK, (ci + 1) * CHUNK)
        dt_c = dt_all[rows, :]
        da_c = da_all[rows, :]
        cum = _sel_left(tril, da_c)
        cum_t = _sel_nt(eye_h, cum)
        dt_t = _sel_nt(eye_h, dt_c)
        last = cum[CHUNK - 1:CHUNK, :]
        ecum_x = _sel_right(jnp.exp(cum), expand)
        wj_x = _sel_right(jnp.exp(last - cum) * dt_c, expand)
        elast_x = _sel_right(jnp.exp(last), expand)

        xs_c = xs_ref[rows, :]
        xs_b = xs_c.astype(BF16)
        xw_b = (xs_c * wj_x).astype(BF16)
        b_c = b_ref[rows, :]
        c_c = c_ref[rows, :]

        for g in range(SSD_GROUPS):
            ns = slice(g * SSD_STATE, (g + 1) * SSD_STATE)
            gs = slice(g * gp, (g + 1) * gp)
            cb = _dot_nt(c_c[:, ns], b_c[:, ns])
            h_g = h_ref[g]
            y_inter = _dot(c_c[:, ns], h_g.astype(BF16)) * ecum_x[:, gs]
            for hp in range(SSD_HEADS_PER_GROUP // 2):
                col = g * gp + hp * LANES
                xpair = xs_b[:, col:col + LANES]
                y_pair = jnp.zeros((CHUNK, LANES), F32)
                for sub in range(2):
                    h = g * SSD_HEADS_PER_GROUP + hp * 2 + sub
                    seg = cum[:, h:h + 1] - cum_t[h:h + 1, :]
                    decay = jnp.exp(jnp.where(causal, seg, -jnp.inf))
                    w = (cb * decay * dt_t[h:h + 1, :]).astype(BF16)
                    keep = lo_half if sub == 0 else jnp.logical_not(lo_half)
                    xh = jnp.where(keep, xpair, jnp.zeros_like(xpair))
                    y_pair = y_pair + _dot(w, xh)
                yacc_ref[rows, col:col + LANES] = (
                    y_pair + y_inter[:, hp * LANES:(hp + 1) * LANES])
            h_ref[g] = h_g * elast_x[:, gs] + _dot_tn(b_c[:, ns], xw_b[:, gs])

    for g in range(SSD_GROUPS):
        gs = slice(g * gp, (g + 1) * gp)
        y = yacc_ref[:, gs] + dexp_ref[:, gs] * xs_ref[:, gs]
        y = y * _silu(z_ref[:, gs].astype(F32))
        ms = jnp.mean(y * y, axis=-1, keepdims=True)
        y_ref[:, gs] = (y * lax.rsqrt(ms + RMS_EPS) * nw_ref[:, gs]).astype(BF16)


def _ssd_core(z, xbc, dt, conv_w, conv_b, dt_bias, a_log, d_exp, norm_w, bsz, seq):
    lb = min(SSD_BLOCK, seq)
    nblk = seq // lb
    row = lambda b, c: (b * nblk + c, 0)
    const = lambda b, c: (0, 0)
    return pl.pallas_call(
        _ssd_kernel,
        grid=(bsz, nblk),
        in_specs=[
            pl.BlockSpec((lb, D_INNER), row),
            pl.BlockSpec((lb, SSD_CONV_DIM), row),
            pl.BlockSpec((lb, SSD_HEADS), row),
            pl.BlockSpec((SSD_CONV, SSD_CONV_DIM), const),
            pl.BlockSpec((1, SSD_CONV_DIM), const),
            pl.BlockSpec((1, SSD_HEADS), const),
            pl.BlockSpec((1, SSD_HEADS), const),
            pl.BlockSpec((1, D_INNER), const),
            pl.BlockSpec((1, D_INNER), const),
        ],
        out_specs=pl.BlockSpec((lb, D_INNER), row),
        out_shape=jax.ShapeDtypeStruct((bsz * seq, D_INNER), BF16),
        scratch_shapes=[
            pltpu.VMEM((SSD_GROUPS, SSD_STATE, SSD_HEADS_PER_GROUP * SSD_HEAD_DIM), F32),
            pltpu.VMEM((8, SSD_CONV_DIM), F32),
            pltpu.VMEM((lb + 8, 512), F32),
            pltpu.VMEM((lb, D_INNER), F32),
            pltpu.VMEM((lb, SSD_BC), BF16),
            pltpu.VMEM((lb, SSD_BC), BF16),
            pltpu.VMEM((lb, D_INNER), F32),
        ],
        compiler_params=_cparams(("parallel", "arbitrary")),
        name="ssd_core",
    )(z, xbc, dt, conv_w, conv_b, dt_bias, a_log, d_exp, norm_w)


def _route(x, rwt_ref, rb_ref):
    logits = _dot_f32_nt(rwt_ref[...], x)
    m = jnp.max(logits, axis=0, keepdims=True)
    e = jnp.exp(logits - m)
    scores = e / jnp.sum(e, axis=0, keepdims=True)
    sel = scores + rb_ref[...]
    s = [sel[i:i + 1, :] for i in range(N_EXPERTS)]
    gsum = []
    for g in range(N_EXPERT_GROUPS):
        v = s[g * EXPERTS_PER_GROUP:(g + 1) * EXPERTS_PER_GROUP]
        best = v[0] + v[1]
        for (i, j) in PAIRS[1:]:
            best = jnp.maximum(best, v[i] + v[j])
        gsum.append(best)
    grp = jnp.zeros_like(gsum[0], dtype=I32)
    best = gsum[0]
    for g in range(1, N_EXPERT_GROUPS):
        better = gsum[g] > best
        grp = jnp.where(better, g, grp)
        best = jnp.where(better, gsum[g], best)
    v = []
    for k in range(EXPERTS_PER_GROUP):
        vk = s[k]
        for g in range(1, N_EXPERT_GROUPS):
            vk = jnp.where(grp == g, s[g * EXPERTS_PER_GROUP + k], vk)
        v.append(vk)
    i1 = jnp.zeros_like(grp)
    best = v[0]
    for k in range(1, EXPERTS_PER_GROUP):
        better = v[k] > best
        i1 = jnp.where(better, k, i1)
        best = jnp.where(better, v[k], best)
    i2 = jnp.full_like(grp, -1)
    best = jnp.full_like(v[0], -jnp.inf)
    for k in range(EXPERTS_PER_GROUP):
        better = jnp.logical_and(i1 != k, jnp.logical_or(v[k] > best, i2 < 0))
        i2 = jnp.where(better, k, i2)
        best = jnp.where(better, v[k], best)
    lo = jnp.minimum(i1, i2)
    hi = jnp.maximum(i1, i2)
    pair = jnp.where(lo == 0, hi - 1, jnp.where(lo == 1, hi + 1, 5))
    return grp * len(PAIRS) + pair


def _proj_ln_route_kernel(a_ref, w_ref, bias_ref, xres_ref, g_ref, b_ref, rwt_ref, rb_ref,
                          xo_ref, cls_ref):
    acc = _dot(a_ref[...], w_ref[...]) + bias_ref[...]
    xres = _rows_load(xres_ref) if xres_ref.shape[-1] == LANES else xres_ref[...]
    v = ALPHA * xres + acc
    out = _layer_norm(v, g_ref[...], b_ref[...])
    _rows_store(xo_ref, out)
    cls_ref[...] = _route(out, rwt_ref, rb_ref)


def _proj_ln_route(a, w, bias, xres, g, b, rwt, rb):
    t, k = a.shape
    tm = min(TM_PROJ, t)
    const = lambda i: (0, 0)
    row_spec = pl.BlockSpec((tm * ROW_SUB, LANES), lambda i: (i, 0))
    xres_spec = row_spec if xres.shape[-1] == LANES else pl.BlockSpec((tm, D_MODEL), lambda i: (i, 0))
    return pl.pallas_call(
        _proj_ln_route_kernel,
        grid=(t // tm,),
        in_specs=[
            pl.BlockSpec((tm, k), lambda i: (i, 0)),
            pl.BlockSpec((k, D_MODEL), const),
            pl.BlockSpec((1, D_MODEL), const),
            xres_spec,
            pl.BlockSpec((1, D_MODEL), const),
            pl.BlockSpec((1, D_MODEL), const),
            pl.BlockSpec((N_EXPERTS, D_MODEL), const),
            pl.BlockSpec((N_EXPERTS, 1), const),
        ],
        out_specs=[
            row_spec,
            pl.BlockSpec((1, tm), lambda i: (0, i)),
        ],
        out_shape=[
            jax.ShapeDtypeStruct((t * ROW_SUB, LANES), F32),
            jax.ShapeDtypeStruct((1, t), I32),
        ],
        compiler_params=_cparams(("parallel",)),
        name="proj_ln_route",
    )(a, w, bias, xres, g, b, rwt, rb)


def _positions_kernel(cls_ref, pos_ref, cnt_ref, run_ref, offs_ref, upper_ref):
    tb = cls_ref.shape[1]
    phase = pl.program_id(0)
    blk = pl.program_id(1)

    @pl.when(jnp.logical_and(phase == 0, blk == 0))
    def _():
        run_ref[...] = jnp.zeros_like(run_ref)
        r = lax.broadcasted_iota(I32, (tb, tb), 0)
        c = lax.broadcasted_iota(I32, (tb, tb), 1)
        upper_ref[...] = (r < c).astype(BF16)

    @pl.when(jnp.logical_and(phase == 1, blk == 0))
    def _():
        counts = run_ref[...]
        cnt_ref[...] = counts.astype(I32)
        padded = jnp.ceil(counts * (1.0 / ROW_TILE)) * ROW_TILE
        r = lax.broadcasted_iota(I32, (CLASS_ROWS, CLASS_ROWS), 0)
        c = lax.broadcasted_iota(I32, (CLASS_ROWS, CLASS_ROWS), 1)
        strict_lower = (c < r).astype(BF16)
        wide = jnp.broadcast_to(padded, (CLASS_ROWS, LANES))
        offs_ref[...] = _sel_left(strict_lower, wide)[:, 0:1]
        run_ref[...] = jnp.zeros_like(run_ref)

    cls = cls_ref[...]
    cid = lax.broadcasted_iota(I32, (CLASS_ROWS, tb), 0)
    onehot = cid == cls
    run = run_ref[...]

    @pl.when(phase == 1)
    def _():
        before = _dot(onehot.astype(BF16), upper_ref[...])
        pos = jnp.sum(jnp.where(onehot, before + (run + offs_ref[...]), 0.0), axis=0, keepdims=True)
        pos_ref[...] = pos.astype(I32)

    run_ref[...] = run + jnp.sum(onehot.astype(F32), axis=1, keepdims=True)


def _sorted_positions(cls):
    t = cls.shape[1]
    tb = min(RANK_BLOCK, t)
    return pl.pallas_call(
        _positions_kernel,
        grid=(2, t // tb),
        in_specs=[pl.BlockSpec((1, tb), lambda p, i: (0, i))],
        out_specs=[
            pl.BlockSpec((1, tb), lambda p, i: (0, i * p)),
            pl.BlockSpec((CLASS_ROWS, 1), lambda p, i: (0, 0)),
        ],
        out_shape=[
            jax.ShapeDtypeStruct((1, t), I32),
            jax.ShapeDtypeStruct((CLASS_ROWS, 1), I32),
        ],
        scratch_shapes=[
            pltpu.VMEM((CLASS_ROWS, 1), F32),
            pltpu.VMEM((CLASS_ROWS, 1), F32),
            pltpu.VMEM((tb, tb), BF16),
        ],
        compiler_params=_cparams(("arbitrary", "arbitrary")),
        name="sorted_positions",
    )(cls)


def _invert_kernel(pos_ref, tok_ref):
    def clear(r, carry):
        tok_ref[r] = 0
        return carry

    lax.fori_loop(0, tok_ref.shape[0], clear, 0, unroll=16)

    def place(t, carry):
        tok_ref[pos_ref[t]] = t
        return carry

    lax.fori_loop(0, pos_ref.shape[0], place, 0, unroll=16)


def _invert_positions(pos, n_rows):
    return pl.pallas_call(
        _invert_kernel,
        in_specs=[pl.BlockSpec(memory_space=pltpu.SMEM)],
        out_specs=pl.BlockSpec(memory_space=pltpu.SMEM),
        out_shape=jax.ShapeDtypeStruct((n_rows,), I32),
        name="invert_positions",
    )(pos)


DMA_UNROLL = 8


def _row_slice(i, n=1):
    return pl.ds(pl.multiple_of(i * ROW_SUB, ROW_SUB), n * ROW_SUB)


def _row_gather(x_hbm, xbuf, sems, tok_ref, tile, slot):
    def copy(r):
        tok = tok_ref[tile * ROW_TILE + r]
        return pltpu.make_async_copy(x_hbm.at[_row_slice(tok)], xbuf.at[slot, _row_slice(r)],
                                     sems.at[slot])
    return copy


def _row_scatter(obuf, out_hbm, sems, tok_ref, tile, slot):
    def copy(r):
        tok = tok_ref[tile * ROW_TILE + r]
        return pltpu.make_async_copy(obuf.at[slot, _row_slice(r)], out_hbm.at[_row_slice(tok)],
                                     sems.at[slot])
    return copy


def _start_rows(copy, n):
    groups = n // DMA_UNROLL

    def body(gi, carry):
        for u in range(DMA_UNROLL):
            copy(gi * DMA_UNROLL + u).start()
        return carry
    lax.fori_loop(0, groups, body, 0)

    def tail(r, carry):
        copy(r).start()
        return carry
    lax.fori_loop(groups * DMA_UNROLL, n, tail, 0)


def _wait_rows(hbm, buf, sem, slot, n, hbm_is_src):
    @pl.when(n > 0)
    def _():
        h, v = hbm.at[_row_slice(0, n)], buf.at[slot, _row_slice(0, n)]
        if hbm_is_src:
            pltpu.make_async_copy(h, v, sem.at[slot]).wait()
        else:
            pltpu.make_async_copy(v, h, sem.at[slot]).wait()


def _ffn_kernel(e1_ref, e2_ref, nrows_ref, tok_ref, x_hbm, wg1_ref, wu1_ref, wd1_ref,
                wg2_ref, wu2_ref, wd2_ref, rw_ref, g_ref, b_ref, out_hbm,
                xbuf, obuf, sem_in, sem_out):
    i = pl.program_id(0)
    nt = pl.num_programs(0)
    slot = i % 2
    n_i = nrows_ref[i]

    @pl.when(i == 0)
    def _():
        xbuf[...] = jnp.zeros_like(xbuf)
        _start_rows(_row_gather(x_hbm, xbuf, sem_in, tok_ref, 0, 0), nrows_ref[0])

    @pl.when(i + 1 < nt)
    def _():
        _start_rows(_row_gather(x_hbm, xbuf, sem_in, tok_ref, i + 1, 1 - slot), nrows_ref[i + 1])

    _wait_rows(x_hbm, xbuf, sem_in, slot, n_i, True)

    @pl.when(i >= 2)
    def _():
        _wait_rows(out_hbm, obuf, sem_out, slot, nrows_ref[i - 2], False)

    @pl.when(n_i > 0)
    def _():
        x = _rows_load(xbuf, (slot,))
        xb = x.astype(BF16)
        logits = _dot_f32(x, rw_ref[...])
        m = jnp.max(logits, axis=-1, keepdims=True)
        e = jnp.exp(logits - m)
        scores = e / jnp.sum(e, axis=-1, keepdims=True)
        lane = lax.broadcasted_iota(I32, scores.shape, 1)
        s1 = jnp.sum(jnp.where(lane == e1_ref[i], scores, 0.0), axis=-1, keepdims=True)
        s2 = jnp.sum(jnp.where(lane == e2_ref[i], scores, 0.0), axis=-1, keepdims=True)
        den = s1 + s2
        h1 = _silu(_dot(xb, wg1_ref[...])) * _dot(xb, wu1_ref[...])
        h2 = _silu(_dot(xb, wg2_ref[...])) * _dot(xb, wu2_ref[...])
        ffn = ((s1 / den) * _dot(h1.astype(BF16), wd1_ref[...])
               + (s2 / den) * _dot(h2.astype(BF16), wd2_ref[...]))
        _rows_store(obuf, _layer_norm(ALPHA * x + ffn, g_ref[...], b_ref[...]), (slot,))
        _start_rows(_row_scatter(obuf, out_hbm, sem_out, tok_ref, i, slot), n_i)

    @pl.when(i == nt - 1)
    def _():
        @pl.when(i >= 1)
        def _():
            _wait_rows(out_hbm, obuf, sem_out, 1 - slot, nrows_ref[i - 1], False)
        _wait_rows(out_hbm, obuf, sem_out, slot, n_i, False)


def _grouped_ffn(x, tok, tile_e1, tile_e2, tile_rows, wg, wu, wd, rw, g, b):
    t = x.shape[0] // ROW_SUB
    rt = ROW_TILE
    nt = tile_rows.shape[0]
    const = lambda i, *_: (0, 0)
    w1 = lambda i, e1, e2, n, tk: (e1[i], 0, 0)
    w2 = lambda i, e1, e2, n, tk: (e2[i], 0, 0)
    grid_spec = pltpu.PrefetchScalarGridSpec(
        num_scalar_prefetch=4,
        grid=(nt,),
        in_specs=[
            pl.BlockSpec(memory_space=pl.ANY),
            pl.BlockSpec((None, D_MODEL, D_FF), w1),
            pl.BlockSpec((None, D_MODEL, D_FF), w1),
            pl.BlockSpec((None, D_FF, D_MODEL), w1),
            pl.BlockSpec((None, D_MODEL, D_FF), w2),
            pl.BlockSpec((None, D_MODEL, D_FF), w2),
            pl.BlockSpec((None, D_FF, D_MODEL), w2),
            pl.BlockSpec((D_MODEL, N_EXPERTS), const),
            pl.BlockSpec((1, D_MODEL), const),
            pl.BlockSpec((1, D_MODEL), const),
        ],
        out_specs=pl.BlockSpec(memory_space=pl.ANY),
        scratch_shapes=[
            pltpu.VMEM((2, rt * ROW_SUB, LANES), F32),
            pltpu.VMEM((2, rt * ROW_SUB, LANES), F32),
            pltpu.SemaphoreType.DMA((2,)),
            pltpu.SemaphoreType.DMA((2,)),
        ],
    )
    return pl.pallas_call(
        _ffn_kernel,
        grid_spec=grid_spec,
        out_shape=jax.ShapeDtypeStruct((t * ROW_SUB, LANES), F32),
        compiler_params=pltpu.CompilerParams(dimension_semantics=("arbitrary",),
                                             vmem_limit_bytes=VMEM_LIMIT, has_side_effects=True),
        name="grouped_ffn",
    )(tile_e1, tile_e2, tile_rows, tok, x, wg, wu, wd, wg, wu, wd, rw, g, b)


def _tile_table(counts, nt):
    rt = ROW_TILE
    padded = ((counts + rt - 1) // rt) * rt
    ends = jnp.cumsum(padded)
    filled = ends - padded + counts
    start = jnp.arange(nt, dtype=I32) * rt
    tile_cls = jnp.sum((start[:, None] >= ends[None, :]).astype(I32), axis=1)
    onehot = tile_cls[:, None] == jnp.arange(N_CLASSES, dtype=I32)[None, :]
    tile_fill = jnp.sum(jnp.where(onehot, filled[None, :], 0), axis=1)
    tile_rows = jnp.clip(tile_fill - start, 0, rt).astype(I32)
    last_cls = jnp.max(jnp.where(tile_rows > 0, tile_cls, 0))
    tile_cls = jnp.where(tile_rows > 0, tile_cls, last_cls)
    grp = tile_cls // len(PAIRS)
    pair = tile_cls % len(PAIRS)
    ge3 = (pair >= 3).astype(I32)
    ge5 = (pair >= 5).astype(I32)
    lo = ge3 + ge5
    hi = pair + 1 - 2 * ge3 - ge5
    return grp * EXPERTS_PER_GROUP + lo, grp * EXPERTS_PER_GROUP + hi, tile_rows


def _moe_layer(x, cls, wg, wu, wd, rw, g, b):
    t = x.shape[0] // ROW_SUB
    nt = t // ROW_TILE + N_CLASSES
    pos, counts = _sorted_positions(cls)
    tok = _invert_positions(pos.reshape(t), nt * ROW_TILE)
    tile_e1, tile_e2, tile_rows = _tile_table(counts[:N_CLASSES, 0], nt)
    return _grouped_ffn(x, tok, tile_e1, tile_e2, tile_rows, wg, wu, wd, rw, g, b)


def _qkv_kernel(x_ref, w_ref, b_ref, q_ref, k_ref, v_ref):
    xb = _rows_load(x_ref).astype(BF16)
    nchunk = 512
    for o, ref in enumerate((q_ref, k_ref, v_ref)):
        for j in range(D_MODEL // nchunk):
            sl = slice(j * nchunk, (j + 1) * nchunk)
            wsl = slice(o * D_MODEL + j * nchunk, o * D_MODEL + (j + 1) * nchunk)
            r = _dot(xb, w_ref[:, wsl]) + b_ref[:, wsl]
            if o == 0:
                r = r * (ATT_HEAD_DIM ** -0.5)
            ref[:, sl] = r.astype(BF16)


def _qkv_proj(xrows, w, b):
    t = xrows.shape[0] // ROW_SUB
    tm = min(TM_PROJ, t)
    const = lambda i: (0, 0)
    row = pl.BlockSpec((tm, D_MODEL), lambda i: (i, 0))
    return pl.pallas_call(
        _qkv_kernel,
        grid=(t // tm,),
        in_specs=[pl.BlockSpec((tm * ROW_SUB, LANES), lambda i: (i, 0)),
                  pl.BlockSpec((D_MODEL, 3 * D_MODEL), const),
                  pl.BlockSpec((1, 3 * D_MODEL), const)],
        out_specs=[row, row, row],
        out_shape=[jax.ShapeDtypeStruct((t, D_MODEL), BF16)] * 3,
        compiler_params=_cparams(("parallel",)),
        name="qkv_proj",
    )(xrows, w, b)


def _attn_kernel(q_ref, *rest):
    k_refs = rest[:ATT_KBLOCKS]
    v_refs = rest[ATT_KBLOCKS:2 * ATT_KBLOCKS]
    bias_ref, o_ref, s_ref, p_ref = rest[2 * ATT_KBLOCKS:]
    qi = pl.program_id(1)
    nkeys = ATT_KBLOCKS * ATT_BLOCK
    grows = ATT_HEAD_GROUP * ATT_BLOCK
    first_valid = jnp.maximum((ATT_KBLOCKS - 1 - qi) * ATT_BLOCK, 0)
    key_ok = lax.broadcasted_iota(I32, (grows, nkeys), 1) >= first_valid
    lane = lax.broadcasted_iota(I32, (ATT_BLOCK, LANES), 1)
    lo_half = lane < ATT_HEAD_DIM
    halves = (lo_half, jnp.logical_not(lo_half))
    blk = lambda i: slice(i * ATT_BLOCK, (i + 1) * ATT_BLOCK)

    for grp in range(ATT_HEADS // ATT_HEAD_GROUP):
        heads = range(grp * ATT_HEAD_GROUP, (grp + 1) * ATT_HEAD_GROUP)
        for h in heads:
            cs = blk(h // 2)
            qp = q_ref[:, cs]
            qm = jnp.where(halves[h % 2], qp, jnp.zeros_like(qp))
            for d in range(ATT_KBLOCKS):
                s_ref[blk(h - heads[0]), blk(d)] = _dot_nt(qm, k_refs[d][:, cs])
        s = s_ref[...] + bias_ref[grp * grows:(grp + 1) * grows, :]
        s = jnp.where(key_ok, s, NEG_BIG)
        m = jnp.max(s, axis=-1, keepdims=True)
        p = jnp.exp(s - m)
        inv_l = 1.0 / jnp.sum(p, axis=-1, keepdims=True)
        p_ref[...] = p.astype(BF16)
        for hp in range(heads[0] // 2, heads[0] // 2 + ATT_HEAD_GROUP // 2):
            cs = blk(hp)
            out_pair = None
            for sub in range(2):
                rows = blk(hp * 2 + sub - heads[0])
                o = _dot(p_ref[rows, blk(0)], v_refs[0][:, cs])
                for d in range(1, ATT_KBLOCKS):
                    o = o + _dot(p_ref[rows, blk(d)], v_refs[d][:, cs])
                o = o * inv_l[rows, :]
                out_pair = o if sub == 0 else jnp.where(lo_half, out_pair, o)
            o_ref[:, cs] = out_pair.astype(BF16)


def _band_attention(q, k, v, bias, bsz, seq):
    nq = seq // ATT_BLOCK
    row = lambda b, i: (b * nq + i, 0)

    def kv_spec(d):
        shift = ATT_KBLOCKS - 1 - d
        return pl.BlockSpec((ATT_BLOCK, D_MODEL),
                            lambda b, i: (b * nq + jnp.maximum(i - shift, 0), 0))

    kspecs = [kv_spec(d) for d in range(ATT_KBLOCKS)]
    return pl.pallas_call(
        _attn_kernel,
        grid=(bsz, nq),
        in_specs=[pl.BlockSpec((ATT_BLOCK, D_MODEL), row)] + kspecs + kspecs + [
            pl.BlockSpec((ATT_HEADS * ATT_BLOCK, ATT_KBLOCKS * ATT_BLOCK), lambda b, i: (0, 0))],
        out_specs=pl.BlockSpec((ATT_BLOCK, D_MODEL), row),
        out_shape=jax.ShapeDtypeStruct((bsz * seq, D_MODEL), BF16),
        scratch_shapes=[
            pltpu.VMEM((ATT_HEAD_GROUP * ATT_BLOCK, ATT_KBLOCKS * ATT_BLOCK), F32),
            pltpu.VMEM((ATT_HEAD_GROUP * ATT_BLOCK, ATT_KBLOCKS * ATT_BLOCK), BF16),
        ],
        compiler_params=_cparams(("parallel", "arbitrary")),
        name="band_attention",
    )(q, *([k] * ATT_KBLOCKS), *([v] * ATT_KBLOCKS), bias)


def _attention_bias(rel_table):
    nkeys = ATT_KBLOCKS * ATT_BLOCK
    r = jnp.arange(ATT_BLOCK)[:, None]
    j = jnp.arange(nkeys)[None, :]
    dist = r + LEFT_CHUNKS * CHUNK - j
    idx = jnp.clip(dist, -MAX_REL, MAX_REL) + MAX_REL
    onehot = (idx[:, :, None] == jnp.arange(2 * MAX_REL + 1)[None, None, :]).astype(F32)
    bias = jnp.einsum("rjk,kh->hrj", onehot, rel_table.astype(F32),
                      precision=lax.Precision.HIGHEST)
    first = r // CHUNK
    in_band = jnp.logical_and(j >= first * CHUNK, j < first * CHUNK + (LEFT_CHUNKS + 1) * CHUNK)
    return jnp.where(in_band[None], bias, NEG_BIG).reshape(ATT_HEADS * ATT_BLOCK, nkeys)


def kernel(x, ssm_w_in, ssm_conv_w, ssm_conv_b, ssm_dt_bias, ssm_a_log, ssm_d, ssm_norm_w, ssm_w_out, att_w_qkv, att_b_qkv, att_rel_bias, att_w_o, att_b_o, router_w, router_bias, moe_w_gate, moe_w_up, moe_w_down, ln_mix_g, ln_mix_b, ln_ffn_g, ln_ffn_b):
    bsz, seq, _ = x.shape
    t = bsz * seq
    xt = x.reshape(t, D_MODEL)
    rwt = router_w.T
    rb = router_bias.reshape(N_EXPERTS, 1)
    row = lambda a: a.reshape(1, -1)
    wg = moe_w_gate.astype(BF16)
    wu = moe_w_up.astype(BF16)
    wd = moe_w_down.astype(BF16)

    w_in = ssm_w_in[0]
    wz = w_in[:, :D_INNER].astype(BF16)
    wxbc = w_in[:, D_INNER:D_INNER + SSD_CONV_DIM].astype(BF16)
    wdt = w_in[:, D_INNER + SSD_CONV_DIM:]
    z, xbc, dt = _in_proj(xt, wz, wxbc, wdt)
    d_exp = jnp.repeat(ssm_d[0], SSD_HEAD_DIM).reshape(1, D_INNER)
    y = _ssd_core(z, xbc, dt, ssm_conv_w[0], row(ssm_conv_b[0]), row(ssm_dt_bias[0]),
                  row(ssm_a_log[0]), d_exp, row(ssm_norm_w[0]), bsz, seq)
    zero_bias = jnp.zeros((1, D_MODEL), F32)
    x1, cls = _proj_ln_route(y, ssm_w_out[0].astype(BF16), zero_bias, xt,
                             row(ln_mix_g[0]), row(ln_mix_b[0]), rwt, rb)
    x2 = _moe_layer(x1, cls, wg[0], wu[0], wd[0], router_w, row(ln_ffn_g[0]), row(ln_ffn_b[0]))

    q, k, v = _qkv_proj(x2, att_w_qkv[0].astype(BF16), row(att_b_qkv[0]))
    bias = _attention_bias(att_rel_bias[0])
    att = _band_attention(q, k, v, bias, bsz, seq)
    x3, cls = _proj_ln_route(att, att_w_o[0].astype(BF16), row(att_b_o[0]), x2,
                             row(ln_mix_g[1]), row(ln_mix_b[1]), rwt, rb)
    x4 = _moe_layer(x3, cls, wg[1], wu[1], wd[1], router_w, row(ln_ffn_g[1]), row(ln_ffn_b[1]))
    return x4.reshape(bsz, seq, D_MODEL)
```

```python
import jax
import jax.numpy as jnp
from jax import lax
from jax.experimental import pallas as pl
from jax.experimental.pallas import tpu as pltpu

F32 = jnp.float32
BF16 = jnp.bfloat16
I32 = jnp.int32

D_MODEL = 1024
DEPTH = 2
CHUNK = 64
D_INNER = 2048
SSD_HEAD_DIM = 64
SSD_HEADS = 32
SSD_GROUPS = 4
SSD_HEADS_PER_GROUP = 8
SSD_STATE = 128
SSD_CONV = 4
SSD_BC = SSD_GROUPS * SSD_STATE
SSD_CONV_DIM = D_INNER + 2 * SSD_BC
ATT_HEADS = 16
ATT_HEAD_DIM = 64
LEFT_CHUNKS = 8
MAX_REL = 128
N_EXPERTS = 16
N_EXPERT_GROUPS = 4
EXPERTS_PER_GROUP = 4
D_FF = 512
ALPHA = (2.0 * DEPTH) ** 0.25
LN_EPS = 1e-5
RMS_EPS = 1e-5

PAIRS = ((0, 1), (0, 2), (0, 3), (1, 2), (1, 3), (2, 3))
N_CLASSES = N_EXPERT_GROUPS * len(PAIRS)
CLASS_ROWS = 32

LANES = 128
ROW_SUB = D_MODEL // LANES
VMEM_LIMIT = 56 * 1024 * 1024

TM_PROJ = 512
SSD_BLOCK = 256
ATT_BLOCK = 128
ATT_KBLOCKS = 5
ATT_HEAD_GROUP = 8
RANK_BLOCK = 1024
ROW_TILE = 256
NEG_BIG = -1e30


def _cparams(sem):
    return pltpu.CompilerParams(dimension_semantics=sem, vmem_limit_bytes=VMEM_LIMIT)


def _split3(a):
    hi = a.astype(BF16)
    r1 = a - hi.astype(F32)
    mid = r1.astype(BF16)
    lo = (r1 - mid.astype(F32)).astype(BF16)
    return hi, mid, lo


def _dot(a, b):
    return jnp.dot(a, b, preferred_element_type=F32)


def _dot_nt(a, b):
    return lax.dot_general(a, b, (((1,), (1,)), ((), ())), preferred_element_type=F32)


def _dot_tn(a, b):
    return lax.dot_general(a, b, (((0,), (0,)), ((), ())), preferred_element_type=F32)


def _sel_right(a, m01):
    hi, mid, lo = _split3(a)
    return _dot(hi, m01) + (_dot(mid, m01) + _dot(lo, m01))


def _sel_left(m01, a):
    hi, mid, lo = _split3(a)
    return _dot(m01, hi) + (_dot(m01, mid) + _dot(m01, lo))


def _sel_nt(m01, a):
    hi, mid, lo = _split3(a)
    return _dot_nt(m01, hi) + (_dot_nt(m01, mid) + _dot_nt(m01, lo))


def _split_weight(w):
    k, n = w.shape
    hi = w.astype(BF16)
    mid = (w - hi.astype(F32)).astype(BF16)
    pad = jnp.zeros((k, LANES - n), BF16)
    return jnp.concatenate([hi, pad, mid, pad], axis=1)


def _dot_precise(x, whm_ref):
    x_hi = x.astype(BF16)
    x_mid = (x - x_hi.astype(F32)).astype(BF16)
    a = _dot(x_hi, whm_ref[...])
    return a[:, :LANES] + a[:, LANES:] + _dot(x_mid, whm_ref[:, 0:LANES])


def _rows_load(ref, lead=()):
    n = ref.shape[-2] // ROW_SUB
    return jnp.concatenate([ref[lead + (pl.ds(s, n, stride=ROW_SUB), slice(None))]
                            for s in range(ROW_SUB)], axis=1)


def _rows_store(ref, v, lead=()):
    n = ref.shape[-2] // ROW_SUB
    for s in range(ROW_SUB):
        ref[lead + (pl.ds(s, n, stride=ROW_SUB), slice(None))] = v[:, s * LANES:(s + 1) * LANES]


def _silu(x):
    hx = 0.5 * x
    return hx + hx * jnp.tanh(hx)


def _layer_norm(v, g, b):
    mu = jnp.mean(v, axis=-1, keepdims=True)
    c = v - mu
    var = jnp.mean(c * c, axis=-1, keepdims=True)
    return c * lax.rsqrt(var + LN_EPS) * g + b


def _in_proj_kernel(x_ref, wz_ref, wxbc_ref, wdt_ref, z_ref, xbc_ref, dt_ref):
    x = x_ref[...]
    xb = x.astype(BF16)
    nchunk = 512
    for j in range(D_INNER // nchunk):
        sl = slice(j * nchunk, (j + 1) * nchunk)
        z_ref[:, sl] = _dot(xb, wz_ref[:, sl]).astype(BF16)
    for j in range(SSD_CONV_DIM // nchunk):
        sl = slice(j * nchunk, (j + 1) * nchunk)
        xbc_ref[:, sl] = _dot(xb, wxbc_ref[:, sl]).astype(BF16)
    dt_ref[...] = _dot_precise(x, wdt_ref)[:, 0:SSD_HEADS]


def _in_proj(x2d, wz, wxbc, wdt):
    t = x2d.shape[0]
    tm = min(TM_PROJ, t)
    const = lambda i: (0, 0)
    return pl.pallas_call(
        _in_proj_kernel,
        grid=(t // tm,),
        in_specs=[
            pl.BlockSpec((tm, D_MODEL), lambda i: (i, 0)),
            pl.BlockSpec((D_MODEL, D_INNER), const),
            pl.BlockSpec((D_MODEL, SSD_CONV_DIM), const),
            pl.BlockSpec((D_MODEL, 2 * LANES), const),
        ],
        out_specs=[
            pl.BlockSpec((tm, D_INNER), lambda i: (i, 0)),
            pl.BlockSpec((tm, SSD_CONV_DIM), lambda i: (i, 0)),
            pl.BlockSpec((tm, SSD_HEADS), lambda i: (i, 0)),
        ],
        out_shape=[
            jax.ShapeDtypeStruct((t, D_INNER), BF16),
            jax.ShapeDtypeStruct((t, SSD_CONV_DIM), BF16),
            jax.ShapeDtypeStruct((t, SSD_HEADS), F32),
        ],
        compiler_params=_cparams(("parallel",)),
        name="ssd_in_proj",
    )(x2d, wz, wxbc, wdt)


def _ssd_kernel(z_ref, xbc_ref, dt_ref, cw_ref, cb_ref, dtb_ref, alog_ref, dexp_ref, nw_ref,
                y_ref, h_ref, carry_ref, xs_ref, b_ref, c_ref, yacc_ref):
    lb = z_ref.shape[0]
    nchunks = lb // CHUNK
    gp = SSD_HEADS_PER_GROUP * SSD_HEAD_DIM

    @pl.when(pl.program_id(1) == 0)
    def _():
        h_ref[...] = jnp.zeros_like(h_ref)
        carry_ref[...] = jnp.zeros_like(carry_ref)

    cchunk = 512
    for j in range(SSD_CONV_DIM // cchunk):
        sl = slice(j * cchunk, (j + 1) * cchunk)
        u = xbc_ref[:, sl].astype(F32)
        ext = jnp.concatenate([carry_ref[:, sl], u], axis=0)
        carry_ref[:, sl] = u[lb - 8:lb, :]
        acc = cb_ref[:, sl] + u * cw_ref[SSD_CONV - 1:SSD_CONV, sl]
        for s in range(1, SSD_CONV):
            k = SSD_CONV - 1 - s
            acc = acc + pltpu.roll(ext, s, axis=0)[8:8 + lb, :] * cw_ref[k:k + 1, sl]
        act = _silu(acc)
        if j < D_INNER // cchunk:
            xs_ref[:, sl] = act
        elif j < (D_INNER + SSD_BC) // cchunk:
            b_ref[...] = act.astype(BF16)
        else:
            c_ref[...] = act.astype(BF16)

    r = lax.broadcasted_iota(I32, (CHUNK, LANES), 0)
    c = lax.broadcasted_iota(I32, (CHUNK, LANES), 1)
    lo_half = c < SSD_HEAD_DIM
    causal2 = r >= jnp.where(lo_half, c, c - SSD_HEAD_DIM)
    tril = (lax.broadcasted_iota(I32, (CHUNK, CHUNK), 0)
            >= lax.broadcasted_iota(I32, (CHUNK, CHUNK), 1)).astype(BF16)
    npairs = SSD_HEADS // 2
    pair_sel = (lax.broadcasted_iota(I32, (npairs, SSD_HEADS), 1) // 2
                == lax.broadcasted_iota(I32, (npairs, SSD_HEADS), 0)).astype(BF16)
    even_head = lax.broadcasted_iota(I32, (CHUNK, SSD_HEADS), 1) % 2 == 0
    expand = ((lax.broadcasted_iota(I32, (SSD_HEADS, D_INNER), 1) // SSD_HEAD_DIM)
              == lax.broadcasted_iota(I32, (SSD_HEADS, D_INNER), 0)).astype(BF16)
    a_neg = -jnp.exp(alog_ref[...])
    dt_bias = dtb_ref[...]

    def pair_rows(v):
        stacked = jnp.concatenate([jnp.where(even_head, v, 0.0), jnp.where(even_head, 0.0, v)], axis=0)
        return _sel_nt(pair_sel, stacked)

    for ci in range(nchunks):
        rows = slice(ci * CHUNK, (ci + 1) * CHUNK)
        dt_c = dt_ref[rows, :] + dt_bias
        dt_c = jnp.maximum(dt_c, 0.0) + jnp.log1p(jnp.exp(-jnp.abs(dt_c)))
        cum = _sel_left(tril, dt_c * a_neg)
        cum_x = _sel_right(cum, expand)
        dt_x = _sel_right(dt_c, expand)
        last_x = cum_x[CHUNK - 1:CHUNK, :]
        ecum_x = jnp.exp(cum_x)
        elast_x = jnp.exp(last_x)
        cum_p = pair_rows(cum)
        dt_p = pair_rows(dt_c)

        xs_c = xs_ref[rows, :]
        xs_b = xs_c.astype(BF16)
        xw_b = (xs_c * (jnp.exp(last_x - cum_x) * dt_x)).astype(BF16)
        b_c = b_ref[rows, :]
        c_c = c_ref[rows, :]

        for g in range(SSD_GROUPS):
            ns = slice(g * SSD_STATE, (g + 1) * SSD_STATE)
            gs = slice(g * gp, (g + 1) * gp)
            b_g = b_c[:, ns]
            cb2 = _dot_nt(c_c[:, ns], jnp.concatenate([b_g, b_g], axis=0))
            h_g = h_ref[g]
            y_inter = _dot(c_c[:, ns], h_g.astype(BF16)) * ecum_x[:, gs]
            for hp in range(SSD_HEADS_PER_GROUP // 2):
                p = g * (SSD_HEADS_PER_GROUP // 2) + hp
                col = p * LANES
                seg = cum_x[:, col:col + LANES] - cum_p[p:p + 1, :]
                decay = jnp.exp(jnp.where(causal2, seg, -jnp.inf))
                w = (cb2 * decay * dt_p[p:p + 1, :]).astype(BF16)
                xpair = xs_b[:, col:col + LANES]
                zero = jnp.zeros_like(xpair)
                xstack = jnp.concatenate([jnp.where(lo_half, xpair, zero),
                                          jnp.where(lo_half, zero, xpair)], axis=0)
                yacc_ref[rows, col:col + LANES] = (
                    _dot(w, xstack) + y_inter[:, hp * LANES:(hp + 1) * LANES])
            h_ref[g] = h_g * elast_x[:, gs] + _dot_tn(b_g, xw_b[:, gs])

    for g in range(SSD_GROUPS):
        gs = slice(g * gp, (g + 1) * gp)
        y = yacc_ref[:, gs] + dexp_ref[:, gs] * xs_ref[:, gs]
        y = y * _silu(z_ref[:, gs].astype(F32))
        ms = jnp.mean(y * y, axis=-1, keepdims=True)
        y_ref[:, gs] = (y * lax.rsqrt(ms + RMS_EPS) * nw_ref[:, gs]).astype(BF16)


def _ssd_core(z, xbc, dt, conv_w, conv_b, dt_bias, a_log, d_exp, norm_w, bsz, seq):
    lb = min(SSD_BLOCK, seq)
    nblk = seq // lb
    row = lambda b, c: (b * nblk + c, 0)
    const = lambda b, c: (0, 0)
    return pl.pallas_call(
        _ssd_kernel,
        grid=(bsz, nblk),
        in_specs=[
            pl.BlockSpec((lb, D_INNER), row),
            pl.BlockSpec((lb, SSD_CONV_DIM), row),
            pl.BlockSpec((lb, SSD_HEADS), row),
            pl.BlockSpec((SSD_CONV, SSD_CONV_DIM), const),
            pl.BlockSpec((1, SSD_CONV_DIM), const),
            pl.BlockSpec((1, SSD_HEADS), const),
            pl.BlockSpec((1, SSD_HEADS), const),
            pl.BlockSpec((1, D_INNER), const),
            pl.BlockSpec((1, D_INNER), const),
        ],
        out_specs=pl.BlockSpec((lb, D_INNER), row),
        out_shape=jax.ShapeDtypeStruct((bsz * seq, D_INNER), BF16),
        scratch_shapes=[
            pltpu.VMEM((SSD_GROUPS, SSD_STATE, SSD_HEADS_PER_GROUP * SSD_HEAD_DIM), F32),
            pltpu.VMEM((8, SSD_CONV_DIM), F32),
            pltpu.VMEM((lb, D_INNER), F32),
            pltpu.VMEM((lb, SSD_BC), BF16),
            pltpu.VMEM((lb, SSD_BC), BF16),
            pltpu.VMEM((lb, D_INNER), F32),
        ],
        compiler_params=_cparams(("parallel", "arbitrary")),
        name="ssd_core",
    )(z, xbc, dt, conv_w, conv_b, dt_bias, a_log, d_exp, norm_w)


def _route(x, rwt_ref, rb_ref):
    logits = _dot_precise(x, rwt_ref).T[0:N_EXPERTS, :]
    m = jnp.max(logits, axis=0, keepdims=True)
    e = jnp.exp(logits - m)
    scores = e / jnp.sum(e, axis=0, keepdims=True)
    sel = scores + rb_ref[...]
    s = [sel[i:i + 1, :] for i in range(N_EXPERTS)]
    gsum = []
    for g in range(N_EXPERT_GROUPS):
        v = s[g * EXPERTS_PER_GROUP:(g + 1) * EXPERTS_PER_GROUP]
        best = v[0] + v[1]
        for (i, j) in PAIRS[1:]:
            best = jnp.maximum(best, v[i] + v[j])
        gsum.append(best)
    grp = jnp.zeros_like(gsum[0], dtype=I32)
    best = gsum[0]
    for g in range(1, N_EXPERT_GROUPS):
        better = gsum[g] > best
        grp = jnp.where(better, g, grp)
        best = jnp.where(better, gsum[g], best)
    v = []
    for k in range(EXPERTS_PER_GROUP):
        vk = s[k]
        for g in range(1, N_EXPERT_GROUPS):
            vk = jnp.where(grp == g, s[g * EXPERTS_PER_GROUP + k], vk)
        v.append(vk)
    i1 = jnp.zeros_like(grp)
    best = v[0]
    for k in range(1, EXPERTS_PER_GROUP):
        better = v[k] > best
        i1 = jnp.where(better, k, i1)
        best = jnp.where(better, v[k], best)
    i2 = jnp.full_like(grp, -1)
    best = jnp.full_like(v[0], -jnp.inf)
    for k in range(EXPERTS_PER_GROUP):
        better = jnp.logical_and(i1 != k, jnp.logical_or(v[k] > best, i2 < 0))
        i2 = jnp.where(better, k, i2)
        best = jnp.where(better, v[k], best)
    lo = jnp.minimum(i1, i2)
    hi = jnp.maximum(i1, i2)
    pair = jnp.where(lo == 0, hi - 1, jnp.where(lo == 1, hi + 1, 5))
    return grp * len(PAIRS) + pair


def _proj_ln_route_kernel(a_ref, w_ref, bias_ref, xres_ref, g_ref, b_ref, rwt_ref, rb_ref,
                          xo_ref, cls_ref):
    acc = _dot(a_ref[...], w_ref[...]) + bias_ref[...]
    xres = _rows_load(xres_ref) if xres_ref.shape[-1] == LANES else xres_ref[...]
    v = ALPHA * xres + acc
    out = _layer_norm(v, g_ref[...], b_ref[...])
    _rows_store(xo_ref, out)
    cls_ref[...] = _route(out, rwt_ref, rb_ref)


def _proj_ln_route(a, w, bias, xres, g, b, rwt, rb):
    t, k = a.shape
    tm = min(TM_PROJ, t)
    const = lambda i: (0, 0)
    row_spec = pl.BlockSpec((tm * ROW_SUB, LANES), lambda i: (i, 0))
    xres_spec = row_spec if xres.shape[-1] == LANES else pl.BlockSpec((tm, D_MODEL), lambda i: (i, 0))
    return pl.pallas_call(
        _proj_ln_route_kernel,
        grid=(t // tm,),
        in_specs=[
            pl.BlockSpec((tm, k), lambda i: (i, 0)),
            pl.BlockSpec((k, D_MODEL), const),
            pl.BlockSpec((1, D_MODEL), const),
            xres_spec,
            pl.BlockSpec((1, D_MODEL), const),
            pl.BlockSpec((1, D_MODEL), const),
            pl.BlockSpec((D_MODEL, 2 * LANES), const),
            pl.BlockSpec((N_EXPERTS, 1), const),
        ],
        out_specs=[
            row_spec,
            pl.BlockSpec((1, tm), lambda i: (0, i)),
        ],
        out_shape=[
            jax.ShapeDtypeStruct((t * ROW_SUB, LANES), F32),
            jax.ShapeDtypeStruct((1, t), I32),
        ],
        compiler_params=_cparams(("parallel",)),
        name="proj_ln_route",
    )(a, w, bias, xres, g, b, rwt, rb)


def _positions_kernel(cls_ref, pos_ref, cnt_ref, run_ref, offs_ref, upper_ref):
    tb = cls_ref.shape[1]
    phase = pl.program_id(0)
    blk = pl.program_id(1)

    @pl.when(jnp.logical_and(phase == 0, blk == 0))
    def _():
        run_ref[...] = jnp.zeros_like(run_ref)
        r = lax.broadcasted_iota(I32, (tb, tb), 0)
        c = lax.broadcasted_iota(I32, (tb, tb), 1)
        upper_ref[...] = (r < c).astype(BF16)

    @pl.when(jnp.logical_and(phase == 1, blk == 0))
    def _():
        counts = run_ref[...]
        cnt_ref[...] = counts.astype(I32)
        padded = jnp.ceil(counts * (1.0 / ROW_TILE)) * ROW_TILE
        r = lax.broadcasted_iota(I32, (CLASS_ROWS, CLASS_ROWS), 0)
        c = lax.broadcasted_iota(I32, (CLASS_ROWS, CLASS_ROWS), 1)
        strict_lower = (c < r).astype(BF16)
        wide = jnp.broadcast_to(padded, (CLASS_ROWS, LANES))
        offs_ref[...] = _sel_left(strict_lower, wide)[:, 0:1]
        run_ref[...] = jnp.zeros_like(run_ref)

    cls = cls_ref[...]
    cid = lax.broadcasted_iota(I32, (CLASS_ROWS, tb), 0)
    onehot = cid == cls
    run = run_ref[...]

    @pl.when(phase == 1)
    def _():
        before = _dot(onehot.astype(BF16), upper_ref[...])
        pos = jnp.sum(jnp.where(onehot, before + (run + offs_ref[...]), 0.0), axis=0, keepdims=True)
        pos_ref[...] = pos.astype(I32)

    run_ref[...] = run + jnp.sum(onehot.astype(F32), axis=1, keepdims=True)


def _sorted_positions(cls):
    t = cls.shape[1]
    tb = min(RANK_BLOCK, t)
    return pl.pallas_call(
        _positions_kernel,
        grid=(2, t // tb),
        in_specs=[pl.BlockSpec((1, tb), lambda p, i: (0, i))],
        out_specs=[
            pl.BlockSpec((1, tb), lambda p, i: (0, i * p)),
            pl.BlockSpec((CLASS_ROWS, 1), lambda p, i: (0, 0)),
        ],
        out_shape=[
            jax.ShapeDtypeStruct((1, t), I32),
            jax.ShapeDtypeStruct((CLASS_ROWS, 1), I32),
        ],
        scratch_shapes=[
            pltpu.VMEM((CLASS_ROWS, 1), F32),
            pltpu.VMEM((CLASS_ROWS, 1), F32),
            pltpu.VMEM((tb, tb), BF16),
        ],
        compiler_params=_cparams(("arbitrary", "arbitrary")),
        name="sorted_positions",
    )(cls)


def _invert_kernel(pos_ref, tok_ref):
    def clear(r, carry):
        tok_ref[r] = 0
        return carry

    lax.fori_loop(0, tok_ref.shape[0], clear, 0, unroll=16)

    def place(t, carry):
        tok_ref[pos_ref[t]] = t
        return carry

    lax.fori_loop(0, pos_ref.shape[0], place, 0, unroll=16)


def _invert_positions(pos, n_rows):
    return pl.pallas_call(
        _invert_kernel,
        in_specs=[pl.BlockSpec(memory_space=pltpu.SMEM)],
        out_specs=pl.BlockSpec(memory_space=pltpu.SMEM),
        out_shape=jax.ShapeDtypeStruct((n_rows,), I32),
        name="invert_positions",
    )(pos)


def _row_slice(i, n=1):
    return pl.ds(pl.multiple_of(i * ROW_SUB, ROW_SUB), n * ROW_SUB)


def _row_gather(x_hbm, xbuf, sems, tok_ref, tile, slot):
    def copy(r):
        tok = tok_ref[tile * ROW_TILE + r]
        return pltpu.make_async_copy(x_hbm.at[_row_slice(tok)], xbuf.at[slot, _row_slice(r)],
                                     sems.at[slot])
    return copy


def _row_scatter(obuf, out_hbm, sems, tok_ref, tile, slot):
    def copy(r):
        tok = tok_ref[tile * ROW_TILE + r]
        return pltpu.make_async_copy(obuf.at[slot, _row_slice(r)], out_hbm.at[_row_slice(tok)],
                                     sems.at[slot])
    return copy


DMA_UNROLL = 32


def _start_rows(copy, n):
    groups = n // DMA_UNROLL

    def body(gi, carry):
        for u in range(DMA_UNROLL):
            copy(gi * DMA_UNROLL + u).start()
        return carry
    lax.fori_loop(0, groups, body, 0)

    def tail(r, carry):
        copy(r).start()
        return carry
    lax.fori_loop(groups * DMA_UNROLL, n, tail, 0)


def _wait_rows(hbm, buf, sem, slot, n, hbm_is_src):
    @pl.when(n > 0)
    def _():
        h, v = hbm.at[_row_slice(0, n)], buf.at[slot, _row_slice(0, n)]
        if hbm_is_src:
            pltpu.make_async_copy(h, v, sem.at[slot]).wait()
        else:
            pltpu.make_async_copy(v, h, sem.at[slot]).wait()


def _ffn_kernel(e1_ref, e2_ref, nrows_ref, tok_ref, x_hbm, wg1_ref, wu1_ref, wd1_ref,
                wg2_ref, wu2_ref, wd2_ref, rw_ref, g_ref, b_ref, out_hbm,
                xbuf, obuf, sem_in, sem_out):
    i = pl.program_id(0)
    nt = pl.num_programs(0)
    slot = i % 2
    n_i = nrows_ref[i]

    @pl.when(i == 0)
    def _():
        xbuf[...] = jnp.zeros_like(xbuf)
        _start_rows(_row_gather(x_hbm, xbuf, sem_in, tok_ref, 0, 0), nrows_ref[0])

    @pl.when(i + 1 < nt)
    def _():
        _start_rows(_row_gather(x_hbm, xbuf, sem_in, tok_ref, i + 1, 1 - slot), nrows_ref[i + 1])

    _wait_rows(x_hbm, xbuf, sem_in, slot, n_i, True)

    @pl.when(i >= 2)
    def _():
        _wait_rows(out_hbm, obuf, sem_out, slot, nrows_ref[i - 2], False)

    @pl.when(n_i > 0)
    def _():
        x = _rows_load(xbuf, (slot,))
        xb = x.astype(BF16)
        logits = _dot_precise(x, rw_ref)[:, 0:N_EXPERTS]
        m = jnp.max(logits, axis=-1, keepdims=True)
        e = jnp.exp(logits - m)
        scores = e / jnp.sum(e, axis=-1, keepdims=True)
        lane = lax.broadcasted_iota(I32, scores.shape, 1)
        s1 = jnp.sum(jnp.where(lane == e1_ref[i], scores, 0.0), axis=-1, keepdims=True)
        s2 = jnp.sum(jnp.where(lane == e2_ref[i], scores, 0.0), axis=-1, keepdims=True)
        den = s1 + s2
        h1 = _silu(_dot(xb, wg1_ref[...])) * _dot(xb, wu1_ref[...])
        h2 = _silu(_dot(xb, wg2_ref[...])) * _dot(xb, wu2_ref[...])
        ffn = ((s1 / den) * _dot(h1.astype(BF16), wd1_ref[...])
               + (s2 / den) * _dot(h2.astype(BF16), wd2_ref[...]))
        _rows_store(obuf, _layer_norm(ALPHA * x + ffn, g_ref[...], b_ref[...]), (slot,))
        _start_rows(_row_scatter(obuf, out_hbm, sem_out, tok_ref, i, slot), n_i)

    @pl.when(i == nt - 1)
    def _():
        @pl.when(i >= 1)
        def _():
            _wait_rows(out_hbm, obuf, sem_out, 1 - slot, nrows_ref[i - 1], False)
        _wait_rows(out_hbm, obuf, sem_out, slot, n_i, False)


def _grouped_ffn(x, tok, tile_e1, tile_e2, tile_rows, wg, wu, wd, rw, g, b):
    t = x.shape[0] // ROW_SUB
    rt = ROW_TILE
    nt = tile_rows.shape[0]
    const = lambda i, *_: (0, 0)
    w1 = lambda i, e1, e2, n, tk: (e1[i], 0, 0)
    w2 = lambda i, e1, e2, n, tk: (e2[i], 0, 0)
    grid_spec = pltpu.PrefetchScalarGridSpec(
        num_scalar_prefetch=4,
        grid=(nt,),
        in_specs=[
            pl.BlockSpec(memory_space=pl.ANY),
            pl.BlockSpec((None, D_MODEL, D_FF), w1),
            pl.BlockSpec((None, D_MODEL, D_FF), w1),
            pl.BlockSpec((None, D_FF, D_MODEL), w1),
            pl.BlockSpec((None, D_MODEL, D_FF), w2),
            pl.BlockSpec((None, D_MODEL, D_FF), w2),
            pl.BlockSpec((None, D_FF, D_MODEL), w2),
            pl.BlockSpec((D_MODEL, 2 * LANES), const),
            pl.BlockSpec((1, D_MODEL), const),
            pl.BlockSpec((1, D_MODEL), const),
        ],
        out_specs=pl.BlockSpec(memory_space=pl.ANY),
        scratch_shapes=[
            pltpu.VMEM((2, rt * ROW_SUB, LANES), F32),
            pltpu.VMEM((2, rt * ROW_SUB, LANES), F32),
            pltpu.SemaphoreType.DMA((2,)),
            pltpu.SemaphoreType.DMA((2,)),
        ],
    )
    return pl.pallas_call(
        _ffn_kernel,
        grid_spec=grid_spec,
        out_shape=jax.ShapeDtypeStruct((t * ROW_SUB, LANES), F32),
        compiler_params=pltpu.CompilerParams(dimension_semantics=("arbitrary",),
                                             vmem_limit_bytes=VMEM_LIMIT, has_side_effects=True),
        name="grouped_ffn",
    )(tile_e1, tile_e2, tile_rows, tok, x, wg, wu, wd, wg, wu, wd, rw, g, b)


def _tile_table(counts, nt):
    rt = ROW_TILE
    padded = ((counts + rt - 1) // rt) * rt
    ends = jnp.cumsum(padded)
    filled = ends - padded + counts
    start = jnp.arange(nt, dtype=I32) * rt
    tile_cls = jnp.sum((start[:, None] >= ends[None, :]).astype(I32), axis=1)
    onehot = tile_cls[:, None] == jnp.arange(N_CLASSES, dtype=I32)[None, :]
    tile_fill = jnp.sum(jnp.where(onehot, filled[None, :], 0), axis=1)
    tile_rows = jnp.clip(tile_fill - start, 0, rt).astype(I32)
    last_cls = jnp.max(jnp.where(tile_rows > 0, tile_cls, 0))
    tile_cls = jnp.where(tile_rows > 0, tile_cls, last_cls)
    grp = tile_cls // len(PAIRS)
    pair = tile_cls % len(PAIRS)
    ge3 = (pair >= 3).astype(I32)
    ge5 = (pair >= 5).astype(I32)
    lo = ge3 + ge5
    hi = pair + 1 - 2 * ge3 - ge5
    return grp * EXPERTS_PER_GROUP + lo, grp * EXPERTS_PER_GROUP + hi, tile_rows


def _moe_layer(x, cls, wg, wu, wd, rw, g, b):
    t = x.shape[0] // ROW_SUB
    nt = t // ROW_TILE + N_CLASSES
    pos, counts = _sorted_positions(cls)
    tok = _invert_positions(pos.reshape(t), nt * ROW_TILE)
    tile_e1, tile_e2, tile_rows = _tile_table(counts[:N_CLASSES, 0], nt)
    return _grouped_ffn(x, tok, tile_e1, tile_e2, tile_rows, wg, wu, wd, rw, g, b)


def _qkv_kernel(x_ref, w_ref, b_ref, q_ref, k_ref, v_ref):
    xb = _rows_load(x_ref).astype(BF16)
    nchunk = 512
    for o, ref in enumerate((q_ref, k_ref, v_ref)):
        for j in range(D_MODEL // nchunk):
            sl = slice(j * nchunk, (j + 1) * nchunk)
            wsl = slice(o * D_MODEL + j * nchunk, o * D_MODEL + (j + 1) * nchunk)
            r = _dot(xb, w_ref[:, wsl]) + b_ref[:, wsl]
            if o == 0:
                r = r * (ATT_HEAD_DIM ** -0.5)
            ref[:, sl] = r.astype(BF16)


def _qkv_proj(xrows, w, b):
    t = xrows.shape[0] // ROW_SUB
    tm = min(TM_PROJ, t)
    const = lambda i: (0, 0)
    row = pl.BlockSpec((tm, D_MODEL), lambda i: (i, 0))
    return pl.pallas_call(
        _qkv_kernel,
        grid=(t // tm,),
        in_specs=[pl.BlockSpec((tm * ROW_SUB, LANES), lambda i: (i, 0)),
                  pl.BlockSpec((D_MODEL, 3 * D_MODEL), const),
                  pl.BlockSpec((1, 3 * D_MODEL), const)],
        out_specs=[row, row, row],
        out_shape=[jax.ShapeDtypeStruct((t, D_MODEL), BF16)] * 3,
        compiler_params=_cparams(("parallel",)),
        name="qkv_proj",
    )(xrows, w, b)


def _attn_kernel(q_ref, *rest):
    k_refs = rest[:ATT_KBLOCKS]
    v_refs = rest[ATT_KBLOCKS:2 * ATT_KBLOCKS]
    bias_ref, o_ref, s_ref, p_ref = rest[2 * ATT_KBLOCKS:]
    qi = pl.program_id(1)
    nkeys = ATT_KBLOCKS * ATT_BLOCK
    grows = ATT_HEAD_GROUP * ATT_BLOCK
    first_valid = jnp.maximum((ATT_KBLOCKS - 1 - qi) * ATT_BLOCK, 0)
    key_ok = lax.broadcasted_iota(I32, (grows, nkeys), 1) >= first_valid
    lane = lax.broadcasted_iota(I32, (ATT_BLOCK, LANES), 1)
    lo_half = lane < ATT_HEAD_DIM
    halves = (lo_half, jnp.logical_not(lo_half))
    blk = lambda i: slice(i * ATT_BLOCK, (i + 1) * ATT_BLOCK)

    for grp in range(ATT_HEADS // ATT_HEAD_GROUP):
        heads = range(grp * ATT_HEAD_GROUP, (grp + 1) * ATT_HEAD_GROUP)
        for h in heads:
            cs = blk(h // 2)
            qp = q_ref[:, cs]
            qm = jnp.where(halves[h % 2], qp, jnp.zeros_like(qp))
            for d in range(ATT_KBLOCKS):
                s_ref[blk(h - heads[0]), blk(d)] = _dot_nt(qm, k_refs[d][:, cs])
        s = s_ref[...] + bias_ref[grp * grows:(grp + 1) * grows, :]
        s = jnp.where(key_ok, s, NEG_BIG)
        m = jnp.max(s, axis=-1, keepdims=True)
        p = jnp.exp(s - m)
        inv_l = 1.0 / jnp.sum(p, axis=-1, keepdims=True)
        p_ref[...] = p.astype(BF16)
        for hp in range(heads[0] // 2, heads[0] // 2 + ATT_HEAD_GROUP // 2):
            cs = blk(hp)
            out_pair = None
            for sub in range(2):
                rows = blk(hp * 2 + sub - heads[0])
                o = _dot(p_ref[rows, blk(0)], v_refs[0][:, cs])
                for d in range(1, ATT_KBLOCKS):
                    o = o + _dot(p_ref[rows, blk(d)], v_refs[d][:, cs])
                o = o * inv_l[rows, :]
                out_pair = o if sub == 0 else jnp.where(lo_half, out_pair, o)
            o_ref[:, cs] = out_pair.astype(BF16)


def _band_attention(q, k, v, bias, bsz, seq):
    nq = seq // ATT_BLOCK
    row = lambda b, i: (b * nq + i, 0)

    def kv_spec(d):
        shift = ATT_KBLOCKS - 1 - d
        return pl.BlockSpec((ATT_BLOCK, D_MODEL),
                            lambda b, i: (b * nq + jnp.maximum(i - shift, 0), 0))

    kspecs = [kv_spec(d) for d in range(ATT_KBLOCKS)]
    return pl.pallas_call(
        _attn_kernel,
        grid=(bsz, nq),
        in_specs=[pl.BlockSpec((ATT_BLOCK, D_MODEL), row)] + kspecs + kspecs + [
            pl.BlockSpec((ATT_HEADS * ATT_BLOCK, ATT_KBLOCKS * ATT_BLOCK), lambda b, i: (0, 0))],
        out_specs=pl.BlockSpec((ATT_BLOCK, D_MODEL), row),
        out_shape=jax.ShapeDtypeStruct((bsz * seq, D_MODEL), BF16),
        scratch_shapes=[
            pltpu.VMEM((ATT_HEAD_GROUP * ATT_BLOCK, ATT_KBLOCKS * ATT_BLOCK), F32),
            pltpu.VMEM((ATT_HEAD_GROUP * ATT_BLOCK, ATT_KBLOCKS * ATT_BLOCK), BF16),
        ],
        compiler_params=_cparams(("parallel", "arbitrary")),
        name="band_attention",
    )(q, *([k] * ATT_KBLOCKS), *([v] * ATT_KBLOCKS), bias)


def _attention_bias(rel_table):
    nkeys = ATT_KBLOCKS * ATT_BLOCK
    r = jnp.arange(ATT_BLOCK)[:, None]
    j = jnp.arange(nkeys)[None, :]
    dist = r + LEFT_CHUNKS * CHUNK - j
    idx = jnp.clip(dist, -MAX_REL, MAX_REL) + MAX_REL
    onehot = (idx[:, :, None] == jnp.arange(2 * MAX_REL + 1)[None, None, :]).astype(F32)
    bias = jnp.einsum("rjk,kh->hrj", onehot, rel_table.astype(F32),
                      precision=lax.Precision.HIGHEST)
    first = r // CHUNK
    in_band = jnp.logical_and(j >= first * CHUNK, j < first * CHUNK + (LEFT_CHUNKS + 1) * CHUNK)
    return jnp.where(in_band[None], bias, NEG_BIG).reshape(ATT_HEADS * ATT_BLOCK, nkeys)


def kernel(x, ssm_w_in, ssm_conv_w, ssm_conv_b, ssm_dt_bias, ssm_a_log, ssm_d, ssm_norm_w, ssm_w_out, att_w_qkv, att_b_qkv, att_rel_bias, att_w_o, att_b_o, router_w, router_bias, moe_w_gate, moe_w_up, moe_w_down, ln_mix_g, ln_mix_b, ln_ffn_g, ln_ffn_b):
    bsz, seq, _ = x.shape
    t = bsz * seq
    xt = x.reshape(t, D_MODEL)
    rw = _split_weight(router_w)
    rb = router_bias.reshape(N_EXPERTS, 1)
    row = lambda a: a.reshape(1, -1)
    wg = moe_w_gate.astype(BF16)
    wu = moe_w_up.astype(BF16)
    wd = moe_w_down.astype(BF16)

    w_in = ssm_w_in[0]
    wz = w_in[:, :D_INNER].astype(BF16)
    wxbc = w_in[:, D_INNER:D_INNER + SSD_CONV_DIM].astype(BF16)
    wdt = _split_weight(w_in[:, D_INNER + SSD_CONV_DIM:])
    z, xbc, dt = _in_proj(xt, wz, wxbc, wdt)
    d_exp = jnp.repeat(ssm_d[0], SSD_HEAD_DIM).reshape(1, D_INNER)
    y = _ssd_core(z, xbc, dt, ssm_conv_w[0], row(ssm_conv_b[0]), row(ssm_dt_bias[0]),
                  row(ssm_a_log[0]), d_exp, row(ssm_norm_w[0]), bsz, seq)
    zero_bias = jnp.zeros((1, D_MODEL), F32)
    x1, cls = _proj_ln_route(y, ssm_w_out[0].astype(BF16), zero_bias, xt,
                             row(ln_mix_g[0]), row(ln_mix_b[0]), rw, rb)
    x2 = _moe_layer(x1, cls, wg[0], wu[0], wd[0], rw, row(ln_ffn_g[0]), row(ln_ffn_b[0]))

    q, k, v = _qkv_proj(x2, att_w_qkv[0].astype(BF16), row(att_b_qkv[0]))
    bias = _attention_bias(att_rel_bias[0])
    att = _band_attention(q, k, v, bias, bsz, seq)
    x3, cls = _proj_ln_route(att, att_w_o[0].astype(BF16), row(att_b_o[0]), x2,
                             row(ln_mix_g[1]), row(ln_mix_b[1]), rw, rb)
    x4 = _moe_layer(x3, cls, wg[1], wu[1], wd[1], rw, row(ln_ffn_g[1]), row(ln_ffn_b[1]))
    return x4.reshape(bsz, seq, D_MODEL)
```

```python
import jax
import jax.numpy as jnp
from jax import lax
from jax.experimental import pallas as pl
from jax.experimental.pallas import tpu as pltpu

F32 = jnp.float32
BF16 = jnp.bfloat16
I32 = jnp.int32

D_MODEL = 1024
DEPTH = 2
CHUNK = 64
D_INNER = 2048
SSD_HEAD_DIM = 64
SSD_HEADS = 32
SSD_GROUPS = 4
SSD_HEADS_PER_GROUP = 8
SSD_STATE = 128
SSD_CONV = 4
SSD_BC = SSD_GROUPS * SSD_STATE
SSD_CONV_DIM = D_INNER + 2 * SSD_BC
ATT_HEADS = 16
ATT_HEAD_DIM = 64
LEFT_CHUNKS = 8
MAX_REL = 128
N_EXPERTS = 16
N_EXPERT_GROUPS = 4
EXPERTS_PER_GROUP = 4
D_FF = 512
ALPHA = (2.0 * DEPTH) ** 0.25
LN_EPS = 1e-5
RMS_EPS = 1e-5

PAIRS = ((0, 1), (0, 2), (0, 3), (1, 2), (1, 3), (2, 3))
N_CLASSES = N_EXPERT_GROUPS * len(PAIRS)
CLASS_ROWS = 32

LANES = 128
ROW_SUB = D_MODEL // LANES
VMEM_LIMIT = 56 * 1024 * 1024

TM_PROJ = 512
SSD_BLOCK = 256
SSD_SEQS = 1
CONV_HALO = 16
ATT_BLOCK = 128
ATT_KBLOCKS = 5
ATT_HEAD_GROUP = 2
RANK_BLOCK = 1024
ROW_TILE = 256
NEG_BIG = -1e30
LOG2E = 1.4426950408889634


def _cparams(sem):
    return pltpu.CompilerParams(dimension_semantics=sem, vmem_limit_bytes=VMEM_LIMIT)


def _split3(a):
    hi = a.astype(BF16)
    r1 = a - hi.astype(F32)
    mid = r1.astype(BF16)
    lo = (r1 - mid.astype(F32)).astype(BF16)
    return hi, mid, lo


def _dot(a, b):
    return jnp.dot(a, b, preferred_element_type=F32)


def _dot_nt(a, b):
    return lax.dot_general(a, b, (((1,), (1,)), ((), ())), preferred_element_type=F32)


def _dot_tn(a, b):
    return lax.dot_general(a, b, (((0,), (0,)), ((), ())), preferred_element_type=F32)


def _sel_right(a, m01):
    hi, mid, lo = _split3(a)
    return _dot(hi, m01) + (_dot(mid, m01) + _dot(lo, m01))


def _sel_left(m01, a):
    hi, mid, lo = _split3(a)
    return _dot(m01, hi) + (_dot(m01, mid) + _dot(m01, lo))


def _sel_nt(m01, a):
    hi, mid, lo = _split3(a)
    return _dot_nt(m01, hi) + (_dot_nt(m01, mid) + _dot_nt(m01, lo))


def _split_weight(w):
    k, n = w.shape
    hi = w.astype(BF16)
    mid = (w - hi.astype(F32)).astype(BF16)
    pad = jnp.zeros((k, LANES - n), BF16)
    return jnp.concatenate([hi, pad, mid, pad], axis=1)


def _dot_precise(x, whm_ref):
    x_hi = x.astype(BF16)
    x_mid = (x - x_hi.astype(F32)).astype(BF16)
    a = _dot(x_hi, whm_ref[...])
    return a[:, :LANES] + a[:, LANES:] + _dot(x_mid, whm_ref[:, 0:LANES])


def _rows_load(ref, lead=()):
    n = ref.shape[-2] // ROW_SUB
    return jnp.concatenate([ref[lead + (pl.ds(s, n, stride=ROW_SUB), slice(None))]
                            for s in range(ROW_SUB)], axis=1)


def _rows_store(ref, v, lead=()):
    n = ref.shape[-2] // ROW_SUB
    for s in range(ROW_SUB):
        ref[lead + (pl.ds(s, n, stride=ROW_SUB), slice(None))] = v[:, s * LANES:(s + 1) * LANES]


def _silu(x):
    hx = 0.5 * x
    return hx + hx * jnp.tanh(hx)


def _layer_norm(v, g, b):
    mu = jnp.mean(v, axis=-1, keepdims=True)
    c = v - mu
    var = jnp.mean(c * c, axis=-1, keepdims=True)
    return c * lax.rsqrt(var + LN_EPS) * g + b


def _in_proj_kernel(x_ref, wz_ref, wxbc_ref, wdt_ref, z_ref, xbc_ref, dt_ref):
    x = x_ref[...]
    xb = x.astype(BF16)
    nchunk = 512
    for j in range(D_INNER // nchunk):
        sl = slice(j * nchunk, (j + 1) * nchunk)
        z_ref[:, sl] = _dot(xb, wz_ref[:, sl]).astype(BF16)
    for j in range(SSD_CONV_DIM // nchunk):
        sl = slice(j * nchunk, (j + 1) * nchunk)
        xbc_ref[:, sl] = _dot(xb, wxbc_ref[:, sl]).astype(BF16)
    dt_ref[...] = _dot_precise(x, wdt_ref)[:, 0:SSD_HEADS]


def _in_proj(x2d, wz, wxbc, wdt):
    t = x2d.shape[0]
    tm = min(TM_PROJ, t)
    const = lambda i: (0, 0)
    return pl.pallas_call(
        _in_proj_kernel,
        grid=(t // tm,),
        in_specs=[
            pl.BlockSpec((tm, D_MODEL), lambda i: (i, 0)),
            pl.BlockSpec((D_MODEL, D_INNER), const),
            pl.BlockSpec((D_MODEL, SSD_CONV_DIM), const),
            pl.BlockSpec((D_MODEL, 2 * LANES), const),
        ],
        out_specs=[
            pl.BlockSpec((tm, D_INNER), lambda i: (i, 0)),
            pl.BlockSpec((tm, SSD_CONV_DIM), lambda i: (i, 0)),
            pl.BlockSpec((tm, SSD_HEADS), lambda i: (i, 0)),
        ],
        out_shape=[
            jax.ShapeDtypeStruct((t, D_INNER), BF16),
            jax.ShapeDtypeStruct((t, SSD_CONV_DIM), BF16),
            jax.ShapeDtypeStruct((t, SSD_HEADS), F32),
        ],
        compiler_params=_cparams(("parallel",)),
        name="ssd_in_proj",
    )(x2d, wz, wxbc, wdt)


def _ssd_kernel(z_all, xbc_all, dt_all, cw_ref, cb_ref, dtb_ref, alog_ref, dexp_ref, nw_ref,
                y_all, h_all, carry_all):
    nseq, lb = z_all.shape[0], z_all.shape[1]
    nchunks = lb // CHUNK
    gp = SSD_HEADS_PER_GROUP * SSD_HEAD_DIM
    cchunk = 512
    win = CHUNK + CONV_HALO

    @pl.when(pl.program_id(1) == 0)
    def _():
        h_all[...] = jnp.zeros_like(h_all)
        carry_all[...] = jnp.zeros_like(carry_all)

    sr = lax.broadcasted_iota(I32, ((SSD_CONV - 1) * CHUNK, win), 0)
    sc = lax.broadcasted_iota(I32, ((SSD_CONV - 1) * CHUNK, win), 1)
    shift_sel = (sc == CONV_HALO + sr % CHUNK - (sr // CHUNK + 1)).astype(BF16)
    r = lax.broadcasted_iota(I32, (CHUNK, LANES), 0)
    c = lax.broadcasted_iota(I32, (CHUNK, LANES), 1)
    lo_half = c < SSD_HEAD_DIM
    causal2 = r >= jnp.where(lo_half, c, c - SSD_HEAD_DIM)
    tril = (lax.broadcasted_iota(I32, (CHUNK, CHUNK), 0)
            >= lax.broadcasted_iota(I32, (CHUNK, CHUNK), 1)).astype(BF16)
    npairs = SSD_HEADS // 2
    pair_sel = (lax.broadcasted_iota(I32, (npairs, SSD_HEADS), 1) // 2
                == lax.broadcasted_iota(I32, (npairs, SSD_HEADS), 0)).astype(BF16)
    even_head = lax.broadcasted_iota(I32, (CHUNK, SSD_HEADS), 1) % 2 == 0
    expand = ((lax.broadcasted_iota(I32, (SSD_HEADS, D_INNER), 1) // SSD_HEAD_DIM)
              == lax.broadcasted_iota(I32, (SSD_HEADS, D_INNER), 0)).astype(BF16)
    a_neg = -jnp.exp(alog_ref[...])
    dt_bias = dtb_ref[...]

    def pair_rows(v):
        stacked = jnp.concatenate([jnp.where(even_head, v, 0.0), jnp.where(even_head, 0.0, v)], axis=0)
        return _sel_nt(pair_sel, stacked)

    def conv_silu(xbc_ref, carry_ref, ci, j):
        sl = slice(j * cchunk, (j + 1) * cchunk)
        if ci == 0:
            window = jnp.concatenate([carry_ref[:, sl], xbc_ref[0:CHUNK, sl]], axis=0)
        else:
            window = xbc_ref[ci * CHUNK - CONV_HALO:(ci + 1) * CHUNK, sl]
        shifted = _dot(shift_sel, window)
        acc = cb_ref[:, sl] + window[CONV_HALO:, :].astype(F32) * cw_ref[SSD_CONV - 1:SSD_CONV, sl]
        for s in range(1, SSD_CONV):
            k = SSD_CONV - 1 - s
            acc = acc + shifted[(s - 1) * CHUNK:s * CHUNK, :] * cw_ref[k:k + 1, sl]
        return _silu(acc)

    for ci, k in [(ci, k) for ci in range(nchunks) for k in range(nseq)]:
        z_ref, xbc_ref, dt_ref, y_ref = z_all.at[k], xbc_all.at[k], dt_all.at[k], y_all.at[k]
        h_ref, carry_ref = h_all.at[k], carry_all.at[k]
        rows = slice(ci * CHUNK, (ci + 1) * CHUNK)
        xs_parts = [conv_silu(xbc_ref, carry_ref, ci, j) for j in range(D_INNER // cchunk)]
        b_c = conv_silu(xbc_ref, carry_ref, ci, D_INNER // cchunk).astype(BF16)
        c_c = conv_silu(xbc_ref, carry_ref, ci, D_INNER // cchunk + 1).astype(BF16)

        dt_c = dt_ref[rows, :] + dt_bias
        dt_c = jnp.maximum(dt_c, 0.0) + jnp.log1p(jnp.exp(-jnp.abs(dt_c)))
        cum = _sel_left(tril, dt_c * a_neg)
        cum_x = _sel_right(cum, expand)
        dt_x = _sel_right(dt_c, expand)
        last_x = cum_x[CHUNK - 1:CHUNK, :]
        cum_p = pair_rows(cum)
        dt_p = pair_rows(dt_c)

        for g in range(SSD_GROUPS):
            ns = slice(g * SSD_STATE, (g + 1) * SSD_STATE)
            gs = slice(g * gp, (g + 1) * gp)
            xs_g = xs_parts[g]
            xs_b = xs_g.astype(BF16)
            cum_g = cum_x[:, gs]
            xw_b = (xs_g * (jnp.exp(last_x[:, gs] - cum_g) * dt_x[:, gs])).astype(BF16)
            b_g = b_c[:, ns]
            cb2 = _dot_nt(c_c[:, ns], jnp.concatenate([b_g, b_g], axis=0))
            h_g = h_ref[g]
            y_inter = _dot(c_c[:, ns], h_g.astype(BF16)) * jnp.exp(cum_g)
            y_parts = []
            for hp in range(SSD_HEADS_PER_GROUP // 2):
                p = g * (SSD_HEADS_PER_GROUP // 2) + hp
                lanes = slice(hp * LANES, (hp + 1) * LANES)
                seg = cum_g[:, lanes] - cum_p[p:p + 1, :]
                decay = jnp.exp(jnp.where(causal2, seg, -jnp.inf))
                w = (cb2 * decay * dt_p[p:p + 1, :]).astype(BF16)
                xpair = xs_b[:, lanes]
                zero = jnp.zeros_like(xpair)
                xstack = jnp.concatenate([jnp.where(lo_half, xpair, zero),
                                          jnp.where(lo_half, zero, xpair)], axis=0)
                y_parts.append(_dot(w, xstack) + y_inter[:, lanes])
            h_ref[g] = h_g * jnp.exp(last_x[:, gs]) + _dot_tn(b_g, xw_b)
            y = jnp.concatenate(y_parts, axis=1) + dexp_ref[:, gs] * xs_g
            y = y * _silu(z_ref[rows, gs].astype(F32))
            ms = jnp.mean(y * y, axis=-1, keepdims=True)
            y_ref[rows, gs] = (y * lax.rsqrt(ms + RMS_EPS) * nw_ref[:, gs]).astype(BF16)

    for k in range(nseq):
        carry_all[k] = xbc_all[k, lb - CONV_HALO:lb, :]


def _ssd_core(z, xbc, dt, conv_w, conv_b, dt_bias, a_log, d_exp, norm_w, bsz, seq):
    lb = min(SSD_BLOCK, seq)
    nblk = seq // lb
    nseq = SSD_SEQS if bsz % SSD_SEQS == 0 else 1
    t = bsz * seq
    split = lambda a: a.reshape(nseq, t // nseq, a.shape[-1])
    row = lambda p, c: (0, p * nblk + c, 0)
    const = lambda p, c: (0, 0)
    out = pl.pallas_call(
        _ssd_kernel,
        grid=(bsz // nseq, nblk),
        in_specs=[
            pl.BlockSpec((nseq, lb, D_INNER), row),
            pl.BlockSpec((nseq, lb, SSD_CONV_DIM), row),
            pl.BlockSpec((nseq, lb, SSD_HEADS), row),
            pl.BlockSpec((SSD_CONV, SSD_CONV_DIM), const),
            pl.BlockSpec((1, SSD_CONV_DIM), const),
            pl.BlockSpec((1, SSD_HEADS), const),
            pl.BlockSpec((1, SSD_HEADS), const),
            pl.BlockSpec((1, D_INNER), const),
            pl.BlockSpec((1, D_INNER), const),
        ],
        out_specs=pl.BlockSpec((nseq, lb, D_INNER), row),
        out_shape=jax.ShapeDtypeStruct((nseq, t // nseq, D_INNER), BF16),
        scratch_shapes=[
            pltpu.VMEM((nseq, SSD_GROUPS, SSD_STATE, SSD_HEADS_PER_GROUP * SSD_HEAD_DIM), F32),
            pltpu.VMEM((nseq, CONV_HALO, SSD_CONV_DIM), BF16),
        ],
        compiler_params=_cparams(("parallel", "arbitrary")),
        name="ssd_core",
    )(split(z), split(xbc), split(dt), conv_w, conv_b, dt_bias, a_log, d_exp, norm_w)
    return out.reshape(t, D_INNER)


def _route(x, rwt_ref, rb_ref):
    logits = _dot_precise(x, rwt_ref).T[0:N_EXPERTS, :]
    m = jnp.max(logits, axis=0, keepdims=True)
    e = jnp.exp(logits - m)
    scores = e / jnp.sum(e, axis=0, keepdims=True)
    sel = scores + rb_ref[...]
    s = [sel[i:i + 1, :] for i in range(N_EXPERTS)]
    gsum = []
    for g in range(N_EXPERT_GROUPS):
        v = s[g * EXPERTS_PER_GROUP:(g + 1) * EXPERTS_PER_GROUP]
        best = v[0] + v[1]
        for (i, j) in PAIRS[1:]:
            best = jnp.maximum(best, v[i] + v[j])
        gsum.append(best)
    grp = jnp.zeros_like(gsum[0], dtype=I32)
    best = gsum[0]
    for g in range(1, N_EXPERT_GROUPS):
        better = gsum[g] > best
        grp = jnp.where(better, g, grp)
        best = jnp.where(better, gsum[g], best)
    v = []
    for k in range(EXPERTS_PER_GROUP):
        vk = s[k]
        for g in range(1, N_EXPERT_GROUPS):
            vk = jnp.where(grp == g, s[g * EXPERTS_PER_GROUP + k], vk)
        v.append(vk)
    i1 = jnp.zeros_like(grp)
    best = v[0]
    for k in range(1, EXPERTS_PER_GROUP):
        better = v[k] > best
        i1 = jnp.where(better, k, i1)
        best = jnp.where(better, v[k], best)
    i2 = jnp.full_like(grp, -1)
    best = jnp.full_like(v[0], -jnp.inf)
    for k in range(EXPERTS_PER_GROUP):
        better = jnp.logical_and(i1 != k, jnp.logical_or(v[k] > best, i2 < 0))
        i2 = jnp.where(better, k, i2)
        best = jnp.where(better, v[k], best)
    lo = jnp.minimum(i1, i2)
    hi = jnp.maximum(i1, i2)
    pair = jnp.where(lo == 0, hi - 1, jnp.where(lo == 1, hi + 1, 5))
    return grp * len(PAIRS) + pair


def _proj_ln_route_kernel(a_ref, w_ref, bias_ref, xres_ref, g_ref, b_ref, rwt_ref, rb_ref,
                          xo_ref, cls_ref):
    acc = _dot(a_ref[...], w_ref[...]) + bias_ref[...]
    xres = _rows_load(xres_ref) if xres_ref.shape[-1] == LANES else xres_ref[...]
    v = ALPHA * xres + acc
    out = _layer_norm(v, g_ref[...], b_ref[...])
    _rows_store(xo_ref, out)
    cls_ref[...] = _route(out, rwt_ref, rb_ref)


def _proj_ln_route(a, w, bias, xres, g, b, rwt, rb):
    t, k = a.shape
    tm = min(TM_PROJ, t)
    const = lambda i: (0, 0)
    row_spec = pl.BlockSpec((tm * ROW_SUB, LANES), lambda i: (i, 0))
    xres_spec = row_spec if xres.shape[-1] == LANES else pl.BlockSpec((tm, D_MODEL), lambda i: (i, 0))
    return pl.pallas_call(
        _proj_ln_route_kernel,
        grid=(t // tm,),
        in_specs=[
            pl.BlockSpec((tm, k), lambda i: (i, 0)),
            pl.BlockSpec((k, D_MODEL), const),
            pl.BlockSpec((1, D_MODEL), const),
            xres_spec,
            pl.BlockSpec((1, D_MODEL), const),
            pl.BlockSpec((1, D_MODEL), const),
            pl.BlockSpec((D_MODEL, 2 * LANES), const),
            pl.BlockSpec((N_EXPERTS, 1), const),
        ],
        out_specs=[
            row_spec,
            pl.BlockSpec((1, tm), lambda i: (0, i)),
        ],
        out_shape=[
            jax.ShapeDtypeStruct((t * ROW_SUB, LANES), F32),
            jax.ShapeDtypeStruct((1, t), I32),
        ],
        compiler_params=_cparams(("parallel",)),
        name="proj_ln_route",
    )(a, w, bias, xres, g, b, rwt, rb)


def _positions_kernel(cls_ref, pos_ref, cnt_ref, run_ref, offs_ref, upper_ref):
    tb = cls_ref.shape[1]
    phase = pl.program_id(0)
    blk = pl.program_id(1)

    @pl.when(jnp.logical_and(phase == 0, blk == 0))
    def _():
        run_ref[...] = jnp.zeros_like(run_ref)
        r = lax.broadcasted_iota(I32, (tb, tb), 0)
        c = lax.broadcasted_iota(I32, (tb, tb), 1)
        upper_ref[...] = (r < c).astype(BF16)

    @pl.when(jnp.logical_and(phase == 1, blk == 0))
    def _():
        counts = run_ref[...]
        cnt_ref[...] = counts.astype(I32)
        padded = jnp.ceil(counts * (1.0 / ROW_TILE)) * ROW_TILE
        r = lax.broadcasted_iota(I32, (CLASS_ROWS, CLASS_ROWS), 0)
        c = lax.broadcasted_iota(I32, (CLASS_ROWS, CLASS_ROWS), 1)
        strict_lower = (c < r).astype(BF16)
        wide = jnp.broadcast_to(padded, (CLASS_ROWS, LANES))
        offs_ref[...] = _sel_left(strict_lower, wide)[:, 0:1]
        run_ref[...] = jnp.zeros_like(run_ref)

    cls = cls_ref[...]
    cid = lax.broadcasted_iota(I32, (CLASS_ROWS, tb), 0)
    onehot = cid == cls
    run = run_ref[...]

    @pl.when(phase == 1)
    def _():
        before = _dot(onehot.astype(BF16), upper_ref[...])
        pos = jnp.sum(jnp.where(onehot, before + (run + offs_ref[...]), 0.0), axis=0, keepdims=True)
        pos_ref[...] = pos.astype(I32)

    run_ref[...] = run + jnp.sum(onehot.astype(F32), axis=1, keepdims=True)


def _sorted_positions(cls):
    t = cls.shape[1]
    tb = min(RANK_BLOCK, t)
    return pl.pallas_call(
        _positions_kernel,
        grid=(2, t // tb),
        in_specs=[pl.BlockSpec((1, tb), lambda p, i: (0, i))],
        out_specs=[
            pl.BlockSpec((1, tb), lambda p, i: (0, i * p)),
            pl.BlockSpec((CLASS_ROWS, 1), lambda p, i: (0, 0)),
        ],
        out_shape=[
            jax.ShapeDtypeStruct((1, t), I32),
            jax.ShapeDtypeStruct((CLASS_ROWS, 1), I32),
        ],
        scratch_shapes=[
            pltpu.VMEM((CLASS_ROWS, 1), F32),
            pltpu.VMEM((CLASS_ROWS, 1), F32),
            pltpu.VMEM((tb, tb), BF16),
        ],
        compiler_params=_cparams(("arbitrary", "arbitrary")),
        name="sorted_positions",
    )(cls)


def _invert_kernel(pos_ref, tok_ref):
    def clear(r, carry):
        tok_ref[r] = 0
        return carry

    lax.fori_loop(0, tok_ref.shape[0], clear, 0, unroll=16)

    def place(t, carry):
        tok_ref[pos_ref[t]] = t
        return carry

    lax.fori_loop(0, pos_ref.shape[0], place, 0, unroll=16)


def _invert_positions(pos, n_rows):
    return pl.pallas_call(
        _invert_kernel,
        in_specs=[pl.BlockSpec(memory_space=pltpu.SMEM)],
        out_specs=pl.BlockSpec(memory_space=pltpu.SMEM),
        out_shape=jax.ShapeDtypeStruct((n_rows,), I32),
        name="invert_positions",
    )(pos)


def _row_slice(i, n=1):
    return pl.ds(pl.multiple_of(i * ROW_SUB, ROW_SUB), n * ROW_SUB)


def _row_gather(x_hbm, xbuf, sems, tok_ref, tile, slot):
    def copy(r):
        tok = tok_ref[tile * ROW_TILE + r]
        return pltpu.make_async_copy(x_hbm.at[_row_slice(tok)], xbuf.at[slot, _row_slice(r)],
                                     sems.at[slot])
    return copy


def _row_scatter(obuf, out_hbm, sems, tok_ref, tile, slot):
    def copy(r):
        tok = tok_ref[tile * ROW_TILE + r]
        return pltpu.make_async_copy(obuf.at[slot, _row_slice(r)], out_hbm.at[_row_slice(tok)],
                                     sems.at[slot])
    return copy


DMA_UNROLL = 32


def _start_rows(copy, n):
    groups = n // DMA_UNROLL

    def body(gi, carry):
        for u in range(DMA_UNROLL):
            copy(gi * DMA_UNROLL + u).start()
        return carry
    lax.fori_loop(0, groups, body, 0)

    def tail(r, carry):
        copy(r).start()
        return carry
    lax.fori_loop(groups * DMA_UNROLL, n, tail, 0)


def _wait_rows(hbm, buf, sem, slot, n, hbm_is_src):
    @pl.when(n > 0)
    def _():
        h, v = hbm.at[_row_slice(0, n)], buf.at[slot, _row_slice(0, n)]
        if hbm_is_src:
            pltpu.make_async_copy(h, v, sem.at[slot]).wait()
        else:
            pltpu.make_async_copy(v, h, sem.at[slot]).wait()


def _ffn_kernel(e1_ref, e2_ref, nrows_ref, tok_ref, x_hbm, wg1_ref, wu1_ref, wd1_ref,
                wg2_ref, wu2_ref, wd2_ref, rw_ref, g_ref, b_ref, out_hbm,
                xbuf, obuf, sem_in, sem_out):
    i = pl.program_id(0)
    nt = pl.num_programs(0)
    slot = i % 2
    n_i = nrows_ref[i]

    @pl.when(i == 0)
    def _():
        xbuf[...] = jnp.zeros_like(xbuf)
        _start_rows(_row_gather(x_hbm, xbuf, sem_in, tok_ref, 0, 0), nrows_ref[0])

    @pl.when(i + 1 < nt)
    def _():
        _start_rows(_row_gather(x_hbm, xbuf, sem_in, tok_ref, i + 1, 1 - slot), nrows_ref[i + 1])

    _wait_rows(x_hbm, xbuf, sem_in, slot, n_i, True)

    @pl.when(i >= 2)
    def _():
        _wait_rows(out_hbm, obuf, sem_out, slot, nrows_ref[i - 2], False)

    @pl.when(n_i > 0)
    def _():
        x = _rows_load(xbuf, (slot,))
        xb = x.astype(BF16)
        logits = _dot_precise(x, rw_ref)[:, 0:N_EXPERTS]
        m = jnp.max(logits, axis=-1, keepdims=True)
        e = jnp.exp(logits - m)
        scores = e / jnp.sum(e, axis=-1, keepdims=True)
        lane = lax.broadcasted_iota(I32, scores.shape, 1)
        s1 = jnp.sum(jnp.where(lane == e1_ref[i], scores, 0.0), axis=-1, keepdims=True)
        s2 = jnp.sum(jnp.where(lane == e2_ref[i], scores, 0.0), axis=-1, keepdims=True)
        den = s1 + s2
        h1 = _silu(_dot(xb, wg1_ref[...])) * _dot(xb, wu1_ref[...])
        h2 = _silu(_dot(xb, wg2_ref[...])) * _dot(xb, wu2_ref[...])
        ffn = ((s1 / den) * _dot(h1.astype(BF16), wd1_ref[...])
               + (s2 / den) * _dot(h2.astype(BF16), wd2_ref[...]))
        _rows_store(obuf, _layer_norm(ALPHA * x + ffn, g_ref[...], b_ref[...]), (slot,))
        _start_rows(_row_scatter(obuf, out_hbm, sem_out, tok_ref, i, slot), n_i)

    @pl.when(i == nt - 1)
    def _():
        @pl.when(i >= 1)
        def _():
            _wait_rows(out_hbm, obuf, sem_out, 1 - slot, nrows_ref[i - 1], False)
        _wait_rows(out_hbm, obuf, sem_out, slot, n_i, False)


def _grouped_ffn(x, tok, tile_e1, tile_e2, tile_rows, wg, wu, wd, rw, g, b):
    t = x.shape[0] // ROW_SUB
    rt = ROW_TILE
    nt = tile_rows.shape[0]
    const = lambda i, *_: (0, 0)
    w1 = lambda i, e1, e2, n, tk: (e1[i], 0, 0)
    w2 = lambda i, e1, e2, n, tk: (e2[i], 0, 0)
    grid_spec = pltpu.PrefetchScalarGridSpec(
        num_scalar_prefetch=4,
        grid=(nt,),
        in_specs=[
            pl.BlockSpec(memory_space=pl.ANY),
            pl.BlockSpec((None, D_MODEL, D_FF), w1),
            pl.BlockSpec((None, D_MODEL, D_FF), w1),
            pl.BlockSpec((None, D_FF, D_MODEL), w1),
            pl.BlockSpec((None, D_MODEL, D_FF), w2),
            pl.BlockSpec((None, D_MODEL, D_FF), w2),
            pl.BlockSpec((None, D_FF, D_MODEL), w2),
            pl.BlockSpec((D_MODEL, 2 * LANES), const),
            pl.BlockSpec((1, D_MODEL), const),
            pl.BlockSpec((1, D_MODEL), const),
        ],
        out_specs=pl.BlockSpec(memory_space=pl.ANY),
        scratch_shapes=[
            pltpu.VMEM((2, rt * ROW_SUB, LANES), F32),
            pltpu.VMEM((2, rt * ROW_SUB, LANES), F32),
            pltpu.SemaphoreType.DMA((2,)),
            pltpu.SemaphoreType.DMA((2,)),
        ],
    )
    return pl.pallas_call(
        _ffn_kernel,
        grid_spec=grid_spec,
        out_shape=jax.ShapeDtypeStruct((t * ROW_SUB, LANES), F32),
        compiler_params=pltpu.CompilerParams(dimension_semantics=("arbitrary",),
                                             vmem_limit_bytes=VMEM_LIMIT, has_side_effects=True),
        name="grouped_ffn",
    )(tile_e1, tile_e2, tile_rows, tok, x, wg, wu, wd, wg, wu, wd, rw, g, b)


def _tile_table(counts, nt):
    rt = ROW_TILE
    padded = ((counts + rt - 1) // rt) * rt
    ends = jnp.cumsum(padded)
    filled = ends - padded + counts
    start = jnp.arange(nt, dtype=I32) * rt
    tile_cls = jnp.sum((start[:, None] >= ends[None, :]).astype(I32), axis=1)
    onehot = tile_cls[:, None] == jnp.arange(N_CLASSES, dtype=I32)[None, :]
    tile_fill = jnp.sum(jnp.where(onehot, filled[None, :], 0), axis=1)
    tile_rows = jnp.clip(tile_fill - start, 0, rt).astype(I32)
    last_cls = jnp.max(jnp.where(tile_rows > 0, tile_cls, 0))
    tile_cls = jnp.where(tile_rows > 0, tile_cls, last_cls)
    grp = tile_cls // len(PAIRS)
    pair = tile_cls % len(PAIRS)
    ge3 = (pair >= 3).astype(I32)
    ge5 = (pair >= 5).astype(I32)
    lo = ge3 + ge5
    hi = pair + 1 - 2 * ge3 - ge5
    return grp * EXPERTS_PER_GROUP + lo, grp * EXPERTS_PER_GROUP + hi, tile_rows


def _moe_layer(x, cls, wg, wu, wd, rw, g, b):
    t = x.shape[0] // ROW_SUB
    nt = t // ROW_TILE + N_CLASSES
    pos, counts = _sorted_positions(cls)
    tok = _invert_positions(pos.reshape(t), nt * ROW_TILE)
    tile_e1, tile_e2, tile_rows = _tile_table(counts[:N_CLASSES, 0], nt)
    return _grouped_ffn(x, tok, tile_e1, tile_e2, tile_rows, wg, wu, wd, rw, g, b)


def _qkv_kernel(x_ref, w_ref, b_ref, q_ref, k_ref, v_ref):
    xb = _rows_load(x_ref).astype(BF16)
    nchunk = 512
    for o, ref in enumerate((q_ref, k_ref, v_ref)):
        for j in range(D_MODEL // nchunk):
            sl = slice(j * nchunk, (j + 1) * nchunk)
            wsl = slice(o * D_MODEL + j * nchunk, o * D_MODEL + (j + 1) * nchunk)
            r = _dot(xb, w_ref[:, wsl]) + b_ref[:, wsl]
            if o == 0:
                r = r * (ATT_HEAD_DIM ** -0.5 * LOG2E)
            ref[:, sl] = r.astype(BF16)


def _qkv_proj(xrows, w, b):
    t = xrows.shape[0] // ROW_SUB
    tm = min(TM_PROJ, t)
    const = lambda i: (0, 0)
    row = pl.BlockSpec((tm, D_MODEL), lambda i: (i, 0))
    return pl.pallas_call(
        _qkv_kernel,
        grid=(t // tm,),
        in_specs=[pl.BlockSpec((tm * ROW_SUB, LANES), lambda i: (i, 0)),
                  pl.BlockSpec((D_MODEL, 3 * D_MODEL), const),
                  pl.BlockSpec((1, 3 * D_MODEL), const)],
        out_specs=[row, row, row],
        out_shape=[jax.ShapeDtypeStruct((t, D_MODEL), BF16)] * 3,
        compiler_params=_cparams(("parallel",)),
        name="qkv_proj",
    )(xrows, w, b)


def _attn_kernel(q_ref, *rest):
    k_refs = rest[:ATT_KBLOCKS]
    v_refs = rest[ATT_KBLOCKS:2 * ATT_KBLOCKS]
    bias_ref, o_ref, s_ref, p_ref = rest[2 * ATT_KBLOCKS:]
    qi = pl.program_id(1)
    nkeys = ATT_KBLOCKS * ATT_BLOCK
    grows = ATT_HEAD_GROUP * ATT_BLOCK
    first_valid = jnp.maximum((ATT_KBLOCKS - 1 - qi) * ATT_BLOCK, 0)
    key_ok = lax.broadcasted_iota(I32, (grows, nkeys), 1) >= first_valid
    lane = lax.broadcasted_iota(I32, (ATT_BLOCK, LANES), 1)
    lo_half = lane < ATT_HEAD_DIM
    halves = (lo_half, jnp.logical_not(lo_half))
    blk = lambda i: slice(i * ATT_BLOCK, (i + 1) * ATT_BLOCK)

    ngroups = ATT_HEADS // ATT_HEAD_GROUP
    group_heads = lambda g: range(g * ATT_HEAD_GROUP, (g + 1) * ATT_HEAD_GROUP)

    def scores(g):
        for h in group_heads(g):
            cs = blk(h // 2)
            qp = q_ref[:, cs]
            qm = jnp.where(halves[h % 2], qp, jnp.zeros_like(qp))
            for d in range(ATT_KBLOCKS):
                s_ref[g % 3, blk(h % ATT_HEAD_GROUP), blk(d)] = _dot_nt(qm, k_refs[d][:, cs])

    def softmax(g):
        s = s_ref[g % 3] + bias_ref[g * grows:(g + 1) * grows, :]
        s = jnp.where(key_ok, s, NEG_BIG)
        m = jnp.max(s, axis=-1, keepdims=True)
        p = jnp.exp2(s - m)
        p_ref[g % 3] = p.astype(BF16)
        return 1.0 / jnp.sum(p, axis=-1, keepdims=True)

    def weighted_values(g, inv_l):
        for hp in range(g * ATT_HEAD_GROUP // 2, (g + 1) * ATT_HEAD_GROUP // 2):
            cs = blk(hp)
            out_pair = None
            for sub in range(2):
                rows = blk((hp * 2 + sub) % ATT_HEAD_GROUP)
                o = _dot(p_ref[g % 3, rows, blk(0)], v_refs[0][:, cs])
                for d in range(1, ATT_KBLOCKS):
                    o = o + _dot(p_ref[g % 3, rows, blk(d)], v_refs[d][:, cs])
                o = o * inv_l[rows, :]
                out_pair = o if sub == 0 else jnp.where(lo_half, out_pair, o)
            o_ref[:, cs] = out_pair.astype(BF16)

    inv = {}
    for step in range(ngroups + 2):
        if step < ngroups:
            scores(step)
        if 1 <= step <= ngroups:
            inv[step - 1] = softmax(step - 1)
        if step >= 2:
            weighted_values(step - 2, inv.pop(step - 2))


def _band_attention(q, k, v, bias, bsz, seq):
    nq = seq // ATT_BLOCK
    row = lambda b, i: (b * nq + i, 0)

    def kv_spec(d):
        shift = ATT_KBLOCKS - 1 - d
        return pl.BlockSpec((ATT_BLOCK, D_MODEL),
                            lambda b, i: (b * nq + jnp.maximum(i - shift, 0), 0))

    kspecs = [kv_spec(d) for d in range(ATT_KBLOCKS)]
    return pl.pallas_call(
        _attn_kernel,
        grid=(bsz, nq),
        in_specs=[pl.BlockSpec((ATT_BLOCK, D_MODEL), row)] + kspecs + kspecs + [
            pl.BlockSpec((ATT_HEADS * ATT_BLOCK, ATT_KBLOCKS * ATT_BLOCK), lambda b, i: (0, 0))],
        out_specs=pl.BlockSpec((ATT_BLOCK, D_MODEL), row),
        out_shape=jax.ShapeDtypeStruct((bsz * seq, D_MODEL), BF16),
        scratch_shapes=[
            pltpu.VMEM((3, ATT_HEAD_GROUP * ATT_BLOCK, ATT_KBLOCKS * ATT_BLOCK), F32),
            pltpu.VMEM((3, ATT_HEAD_GROUP * ATT_BLOCK, ATT_KBLOCKS * ATT_BLOCK), BF16),
        ],
        compiler_params=_cparams(("parallel", "arbitrary")),
        name="band_attention",
    )(q, *([k] * ATT_KBLOCKS), *([v] * ATT_KBLOCKS), bias)


def _attention_bias(rel_table):
    nkeys = ATT_KBLOCKS * ATT_BLOCK
    r = jnp.arange(ATT_BLOCK)[:, None]
    j = jnp.arange(nkeys)[None, :]
    dist = r + LEFT_CHUNKS * CHUNK - j
    idx = jnp.clip(dist, -MAX_REL, MAX_REL) + MAX_REL
    onehot = (idx[:, :, None] == jnp.arange(2 * MAX_REL + 1)[None, None, :]).astype(F32)
    bias = jnp.einsum("rjk,kh->hrj", onehot, rel_table.astype(F32),
                      precision=lax.Precision.HIGHEST)
    first = r // CHUNK
    in_band = jnp.logical_and(j >= first * CHUNK, j < first * CHUNK + (LEFT_CHUNKS + 1) * CHUNK)
    return jnp.where(in_band[None], bias * LOG2E, NEG_BIG).reshape(ATT_HEADS * ATT_BLOCK, nkeys)


def _rows_to_plain_kernel(x_ref, o_ref):
    o_ref[...] = _rows_load(x_ref)


def _rows_to_plain(xrows):
    t = xrows.shape[0] // ROW_SUB
    tm = min(TM_PROJ, t)
    return pl.pallas_call(
        _rows_to_plain_kernel,
        grid=(t // tm,),
        in_specs=[pl.BlockSpec((tm * ROW_SUB, LANES), lambda i: (i, 0))],
        out_specs=pl.BlockSpec((tm, D_MODEL), lambda i: (i, 0)),
        out_shape=jax.ShapeDtypeStruct((t, D_MODEL), xrows.dtype),
        compiler_params=_cparams(("parallel",)),
        name="rows_to_plain",
    )(xrows)


def kernel(x, ssm_w_in, ssm_conv_w, ssm_conv_b, ssm_dt_bias, ssm_a_log, ssm_d, ssm_norm_w, ssm_w_out, att_w_qkv, att_b_qkv, att_rel_bias, att_w_o, att_b_o, router_w, router_bias, moe_w_gate, moe_w_up, moe_w_down, ln_mix_g, ln_mix_b, ln_ffn_g, ln_ffn_b):
    bsz, seq, _ = x.shape
    t = bsz * seq
    xt = x.reshape(t, D_MODEL)
    rw = _split_weight(router_w)
    rb = router_bias.reshape(N_EXPERTS, 1)
    row = lambda a: a.reshape(1, -1)
    wg = moe_w_gate.astype(BF16)
    wu = moe_w_up.astype(BF16)
    wd = moe_w_down.astype(BF16)

    w_in = ssm_w_in[0]
    wz = w_in[:, :D_INNER].astype(BF16)
    wxbc = w_in[:, D_INNER:D_INNER + SSD_CONV_DIM].astype(BF16)
    wdt = _split_weight(w_in[:, D_INNER + SSD_CONV_DIM:])
    z, xbc, dt = _in_proj(xt, wz, wxbc, wdt)
    d_exp = jnp.repeat(ssm_d[0], SSD_HEAD_DIM).reshape(1, D_INNER)
    y = _ssd_core(z, xbc, dt, ssm_conv_w[0], row(ssm_conv_b[0]), row(ssm_dt_bias[0]),
                  row(ssm_a_log[0]), d_exp, row(ssm_norm_w[0]), bsz, seq)
    zero_bias = jnp.zeros((1, D_MODEL), F32)
    x1, cls = _proj_ln_route(y, ssm_w_out[0].astype(BF16), zero_bias, xt,
                             row(ln_mix_g[0]), row(ln_mix_b[0]), rw, rb)
    x2 = _moe_layer(x1, cls, wg[0], wu[0], wd[0], rw, row(ln_ffn_g[0]), row(ln_ffn_b[0]))

    q, k, v = _qkv_proj(x2, att_w_qkv[0].astype(BF16), row(att_b_qkv[0]))
    bias = _attention_bias(att_rel_bias[0])
    att = _band_attention(q, k, v, bias, bsz, seq)
    x3, cls = _proj_ln_route(att, att_w_o[0].astype(BF16), row(att_b_o[0]), x2,
                             row(ln_mix_g[1]), row(ln_mix_b[1]), rw, rb)
    x4 = _moe_layer(x3, cls, wg[1], wu[1], wd[1], rw, row(ln_ffn_g[1]), row(ln_ffn_b[1]))
    return _rows_to_plain(x4).reshape(bsz, seq, D_MODEL)
```
